```python
import math
import jax, jax.numpy as jnp
from jax import lax
import numpy as np

D_MODEL = 1024
BATCH = 8
SEQ = 4096
DEPTH = 2

HEAD_DIM = 64
A_HEADS = 4
A_QK_DIM = HEAD_DIM // 2
B_HEADS = 6
B_PATTERNS = ((128, 1), (512, 4), (2048, 16))
C_HEADS = 6
C_KV_HEADS = 2
C_HALF_WINDOW = 128
C_BLOCK = 128
Q_BLOCK = 128
N_ATTN_HEADS = A_HEADS + B_HEADS + C_HEADS
FFN_HIDDEN = -(-(8 * D_MODEL) // (3 * 256)) * 256
LN_EPS = 1e-5
NEG = -1e30

A_Q = A_HEADS * 2 * A_QK_DIM
A_K = A_HEADS * 2 * A_QK_DIM
A_V = A_HEADS * HEAD_DIM
B_Q = B_HEADS * HEAD_DIM
B_K = B_HEADS * HEAD_DIM
B_V = B_HEADS * HEAD_DIM
C_Q = C_HEADS * HEAD_DIM
C_K = C_KV_HEADS * HEAD_DIM
C_V = C_KV_HEADS * HEAD_DIM
IN_SPLITS = (A_Q, A_K, A_V, B_Q, B_K, B_V, C_Q, C_K, C_V)
IN_WIDTH = A_Q + A_K + A_V + B_Q + B_K + B_V + C_Q + C_K + C_V
VALUE_SLOTS = (2, 5, 8)

kernel_name = "hybrid_diff_dilated_swa_deepnorm_encoder"


def split_points():
    pts, acc = [], 0
    for w in IN_SPLITS[:-1]:
        acc += w
        pts.append(acc)
    return pts


def alibi_slopes():
    s = (2.0 ** (-8.0 * np.arange(1, N_ATTN_HEADS + 1) / N_ATTN_HEADS)).astype(np.float32)
    s_c = jnp.asarray(s[:C_HEADS])
    s_a = jnp.asarray(s[C_HEADS:C_HEADS + A_HEADS])
    s_b = jnp.asarray(s[C_HEADS + A_HEADS:])
    return s_a, s_b, s_c


def layer_norm(x, g, b):
    xf = x.astype(jnp.float32)
    mu = jnp.mean(xf, axis=-1, keepdims=True)
    xc = xf - mu
    var = jnp.mean(xc * xc, axis=-1, keepdims=True)
    return (xc * lax.rsqrt(var + LN_EPS) * g + b).astype(x.dtype)


def rms_norm(x, g):
    xf = x.astype(jnp.float32)
    return xf * lax.rsqrt(jnp.mean(xf * xf, axis=-1, keepdims=True) + LN_EPS) * g


def banded_attention_stats(q, k, v, half, blk, slopes, dist_scale):
    bsz, g, r, L, dh = q.shape
    nb = -(-L // blk)
    lp = nb * blk
    qb = jnp.pad(q, ((0, 0), (0, 0), (0, 0), (0, lp - L), (0, 0))).reshape(bsz, g, r, nb, blk, dh)
    pad = ((0, 0), (0, 0), (blk, lp - L + blk), (0, 0))
    kp = jnp.pad(k, pad)
    vp = jnp.pad(v, pad)

    def band(a):
        return jnp.concatenate(
            [a[:, :, o:o + lp].reshape(bsz, g, nb, blk, dh) for o in (0, blk, 2 * blk)], axis=3)

    kb = band(kp)
    vb = band(vp).astype(jnp.float32)
    s = jnp.einsum('bgrnqd,bgnkd->bgrnqk', qb, kb).astype(jnp.float32) * (dh ** -0.5)
    blk_start = jnp.arange(nb)[:, None] * blk
    qpos = blk_start + jnp.arange(blk)[None]
    kpos = blk_start - blk + jnp.arange(3 * blk)[None]
    dist = jnp.abs(qpos[:, :, None] - kpos[:, None, :])
    valid = (dist <= half) & (kpos >= 0)[:, None, :] & (kpos < L)[:, None, :]
    bias = -slopes.astype(jnp.float32)[:, :, None, None, None] * (dist.astype(jnp.float32) * dist_scale)
    s = jnp.where(valid, s + bias, NEG)
    m = jnp.max(s, axis=-1)
    p = jnp.exp(s - m[..., None])
    l = jnp.sum(p, axis=-1)
    o = jnp.einsum('bgrnqk,bgnkd->bgrnqd', p, vb)
    m = m.reshape(bsz, g, r, lp)[..., :L]
    l = l.reshape(bsz, g, r, lp)[..., :L]
    o = o.reshape(bsz, g, r, lp, dh)[:, :, :, :L]
    return m, l, o


def diff_attention(q, k, v, lam, slopes, subln_g, lambda_init):
    bsz, h, T, _, dq = q.shape
    dv = v.shape[-1]
    nb = T // Q_BLOCK
    qb = q.reshape(bsz, h, nb, Q_BLOCK, 2, dq).transpose(2, 0, 1, 3, 4, 5)
    vf = v.astype(jnp.float32)
    kpos = jnp.arange(T)
    sl = slopes.astype(jnp.float32)[None, :, None, None, None]

    def block(args):
        qi, i = args
        s = jnp.einsum('bhqmd,bhkmd->bhmqk', qi, k).astype(jnp.float32) * (dq ** -0.5)
        qpos = i * Q_BLOCK + jnp.arange(Q_BLOCK)
        dist = jnp.abs(qpos[:, None] - kpos[None, :]).astype(jnp.float32)
        p = jax.nn.softmax(s - sl * dist, axis=-1)
        w = p[:, :, 0] - lam * p[:, :, 1]
        return jnp.einsum('bhqk,bhkd->bhqd', w, vf)

    o = lax.map(block, (qb, jnp.arange(nb)))
    o = o.transpose(1, 2, 0, 3, 4).reshape(bsz, h, T, dv)
    return rms_norm(o, subln_g) * (1.0 - lambda_init)


def dilated_attention(q, k, v, slopes):
    bsz, h, T, dh = q.shape
    ms, ls, os_ = [], [], []
    for window, dil in B_PATTERNS:
        half = (window // 2) // dil
        L = T // dil

        def stride(a):
            return a.reshape(bsz, h, L, dil, dh).transpose(0, 1, 3, 2, 4).reshape(bsz, h * dil, L, dh)

        m, l, o = banded_attention_stats(stride(q)[:, :, None], stride(k), stride(v), half, half,
                                         jnp.repeat(slopes, dil)[:, None], float(dil))
        ms.append(m[:, :, 0].reshape(bsz, h, dil, L).transpose(0, 1, 3, 2).reshape(bsz, h, T))
        ls.append(l[:, :, 0].reshape(bsz, h, dil, L).transpose(0, 1, 3, 2).reshape(bsz, h, T))
        os_.append(o[:, :, 0].reshape(bsz, h, dil, L, dh).transpose(0, 1, 3, 2, 4).reshape(bsz, h, T, dh))
    m_all = jnp.stack(ms)
    l_all = jnp.stack(ls)
    o_all = jnp.stack(os_)
    w = jnp.exp(m_all - jnp.max(m_all, axis=0, keepdims=True))
    return jnp.sum(w[..., None] * o_all, axis=0) / jnp.sum(w * l_all, axis=0)[..., None]


def sink_window_gqa(q, k, v, sink, slopes):
    bsz, T, _ = q.shape
    rep = C_HEADS // C_KV_HEADS
    q = q.reshape(bsz, T, C_KV_HEADS, rep, HEAD_DIM).transpose(0, 2, 3, 1, 4)
    k = k.reshape(bsz, T, C_KV_HEADS, HEAD_DIM).transpose(0, 2, 1, 3)
    v = v.reshape(bsz, T, C_KV_HEADS, HEAD_DIM).transpose(0, 2, 1, 3)
    m, l, o = banded_attention_stats(q, k, v, C_HALF_WINDOW, C_BLOCK,
                                     slopes.reshape(C_KV_HEADS, rep), 1.0)
    sk = sink.astype(jnp.float32).reshape(C_KV_HEADS, rep)[None, :, :, None]
    mx = jnp.maximum(m, sk)
    a = jnp.exp(m - mx)
    out = o * a[..., None] / (l * a + jnp.exp(sk - mx))[..., None]
    return out.transpose(0, 3, 1, 2, 4).reshape(bsz, T, C_HEADS * HEAD_DIM)


def token_mixer(h, w_in, lam, subln_g, sink, w_out, lambda_init):
    bsz, T, _ = h.shape
    proj = h @ w_in
    qa, ka, va, qb, kb, vb, qc, kc, vc = jnp.split(proj, split_points(), axis=-1)
    s_a, s_b, s_c = alibi_slopes()

    qa = qa.reshape(bsz, T, A_HEADS, 2, A_QK_DIM).transpose(0, 2, 1, 3, 4)
    ka = ka.reshape(bsz, T, A_HEADS, 2, A_QK_DIM).transpose(0, 2, 1, 3, 4)
    va = va.reshape(bsz, T, A_HEADS, HEAD_DIM).transpose(0, 2, 1, 3)
    lf = lam.astype(jnp.float32)
    lam_full = jnp.exp(jnp.sum(lf[0] * lf[1])) - jnp.exp(jnp.sum(lf[2] * lf[3])) + lambda_init
    oa = diff_attention(qa, ka, va, lam_full, s_a, subln_g.astype(jnp.float32), lambda_init)
    oa = oa.transpose(0, 2, 1, 3).reshape(bsz, T, A_V)

    def heads(a):
        return a.reshape(bsz, T, B_HEADS, HEAD_DIM).transpose(0, 2, 1, 3)
    ob = dilated_attention(heads(qb), heads(kb), heads(vb), s_b)
    ob = ob.transpose(0, 2, 1, 3).reshape(bsz, T, B_V)

    oc = sink_window_gqa(qc, kc, vc, sink, s_c)

    mixed = jnp.concatenate([oa, ob, oc], axis=-1).astype(h.dtype)
    return mixed @ w_out


def swiglu(h, w_gu, w_down):
    g, u = jnp.split(h @ w_gu, 2, axis=-1)
    return (jax.nn.silu(g) * u) @ w_down


def setup_inputs(seed: int = 0) -> dict:
    key = jax.random.key(seed)
    ks = jax.random.split(key, 13)
    beta = (8 * DEPTH) ** -0.25
    f32 = jnp.float32
    x = jax.random.normal(ks[0], (BATCH, SEQ, D_MODEL), f32)
    c = jax.random.normal(ks[1], (BATCH, D_MODEL), f32)
    w_ada = jax.random.normal(ks[2], (DEPTH, D_MODEL, 6 * D_MODEL), f32) * (0.1 * D_MODEL ** -0.5)
    b_ada = 0.02 * jax.random.normal(ks[3], (DEPTH, 6 * D_MODEL), f32)
    col_scale = np.ones((IN_WIDTH,), np.float32)
    off = 0
    for i, w in enumerate(IN_SPLITS):
        if i in VALUE_SLOTS:
            col_scale[off:off + w] = beta
        off += w
    w_in = jax.random.normal(ks[4], (DEPTH, D_MODEL, IN_WIDTH), f32) * (D_MODEL ** -0.5) * jnp.asarray(col_scale)
    lam = 0.1 * jax.random.normal(ks[5], (DEPTH, 4, A_QK_DIM), f32)
    subln_g = 1.0 + 0.02 * jax.random.normal(ks[6], (DEPTH, HEAD_DIM), f32)
    sink = 0.5 * jax.random.normal(ks[7], (DEPTH, C_HEADS), f32)
    w_out = jax.random.normal(ks[8], (DEPTH, D_MODEL, D_MODEL), f32) * (D_MODEL ** -0.5) * beta
    ln_g = 1.0 + 0.02 * jax.random.normal(ks[9], (DEPTH, 2, D_MODEL), f32)
    ln_b = 0.02 * jax.random.normal(ks[10], (DEPTH, 2, D_MODEL), f32)
    w_gu = jax.random.normal(ks[11], (DEPTH, D_MODEL, 2 * FFN_HIDDEN), f32) * (D_MODEL ** -0.5) * beta
    w_down = jax.random.normal(ks[12], (DEPTH, FFN_HIDDEN, D_MODEL), f32) * (FFN_HIDDEN ** -0.5) * beta
    return {"x": x, "c": c, "w_ada": w_ada, "b_ada": b_ada, "w_in": w_in, "lam": lam,
            "subln_g": subln_g, "sink": sink, "w_out": w_out, "ln_g": ln_g, "ln_b": ln_b,
            "w_gu": w_gu, "w_down": w_down}


def reference(x, c, w_ada, b_ada, w_in, lam, subln_g, sink, w_out, ln_g, ln_b, w_gu, w_down):
    alpha = (2 * DEPTH) ** 0.25
    for layer in range(DEPTH):
        lambda_init = 0.8 - 0.6 * math.exp(-0.3 * layer)
        mod = jax.nn.silu(c) @ w_ada[layer] + b_ada[layer]
        sh1, sc1, g1, sh2, sc2, g2 = [m[:, None, :] for m in jnp.split(mod, 6, axis=-1)]
        h = x * (1.0 + sc1) + sh1
        mix = token_mixer(h, w_in[layer], lam[layer], subln_g[layer], sink[layer], w_out[layer], lambda_init)
        x = layer_norm(alpha * x + (1.0 + g1) * mix, ln_g[layer, 0], ln_b[layer, 0])
        h = x * (1.0 + sc2) + sh2
        x = layer_norm(alpha * x + (1.0 + g2) * swiglu(h, w_gu[layer], w_down[layer]), ln_g[layer, 1], ln_b[layer, 1])
    return x
```

```python
import functools
import math

import numpy as np
import jax
import jax.numpy as jnp
from jax import lax
from jax.experimental import pallas as pl
from jax.experimental.pallas import tpu as pltpu

F32 = jnp.float32
BF16 = jnp.bfloat16

HEAD_DIM = 64
A_HEADS = 4
A_QK_DIM = 32
B_HEADS = 6
B_DILATIONS = (1, 4, 16)
B_HALF = 64
C_HEADS = 6
C_KV_HEADS = 2
C_HALF_WINDOW = 128
N_ATTN_HEADS = 16
LN_EPS = 1e-5
NEG = -1e30

LANES = 128
A_W = A_HEADS * HEAD_DIM
B_W = B_HEADS * HEAD_DIM
C_QW = C_HEADS * HEAD_DIM
C_KW = C_KV_HEADS * HEAD_DIM
IN_WIDTH = 3 * A_W + 3 * B_W + C_QW + 2 * C_KW
VMEM_LIMIT = 56 * 1024 * 1024

_S = 2.0 ** (-8.0 * np.arange(1, N_ATTN_HEADS + 1) / N_ATTN_HEADS)
SLOPES_C = [float(v) for v in _S[:C_HEADS]]
SLOPES_A = [float(v) for v in _S[C_HEADS:C_HEADS + A_HEADS]]
SLOPES_B = [float(v) for v in _S[C_HEADS + A_HEADS:]]


def _nt_dot(a, b):
    return lax.dot_general(a, b, (((1,), (1,)), ((), ())), preferred_element_type=F32)


def _lane_iota(shape):
    return lax.broadcasted_iota(jnp.int32, shape, len(shape) - 1)


def _mod_kernel(c_ref, w_ref, b_ref, o_ref):
    c = c_ref[...]
    a = (c * jax.nn.sigmoid(c)).astype(BF16)
    o_ref[0] = jnp.dot(a, w_ref[0].astype(BF16), preferred_element_type=F32) + b_ref[0]


def _modulation(c, w_ada, b_ada):
    depth, d, n = w_ada.shape
    bsz = c.shape[0]
    tn = 1536
    return pl.pallas_call(
        _mod_kernel,
        grid=(depth, n // tn),
        in_specs=[pl.BlockSpec((bsz, d), lambda l, j: (0, 0)),
                  pl.BlockSpec((1, d, tn), lambda l, j: (l, 0, j)),
                  pl.BlockSpec((1, 1, tn), lambda l, j: (l, 0, j))],
        out_specs=pl.BlockSpec((1, bsz, tn), lambda l, j: (l, 0, j)),
        out_shape=jax.ShapeDtypeStruct((depth, bsz, n), F32),
        compiler_params=pltpu.CompilerParams(dimension_semantics=("arbitrary", "arbitrary"),
                                             vmem_limit_bytes=VMEM_LIMIT),
        name="adaln_mod",
    )(c, w_ada, b_ada.reshape(depth, 1, n))


def _proj_kernel(x_ref, mod_ref, w_ref, cs_ref, o_ref, *, tn):
    sh = mod_ref[0, 0:1, :]
    sc = mod_ref[0, 1:2, :]
    h = (x_ref[0] * (1.0 + sc) + sh).astype(BF16)
    for n0 in range(0, o_ref.shape[2], tn):
        acc = jnp.dot(h, w_ref[:, n0:n0 + tn], preferred_element_type=F32)
        o_ref[0, :, n0:n0 + tn] = (acc * cs_ref[:, n0:n0 + tn]).astype(BF16)


def _in_projection(x, mod6, w_in_bf, col_scale, tm=512):
    bsz, t, d = x.shape
    n = w_in_bf.shape[1]
    return pl.pallas_call(
        functools.partial(_proj_kernel, tn=512),
        grid=(bsz, t // tm),
        in_specs=[pl.BlockSpec((1, tm, d), lambda b, i: (b, i, 0)),
                  pl.BlockSpec((1, 6, d), lambda b, i: (b, 0, 0)),
                  pl.BlockSpec((d, n), lambda b, i: (0, 0), pipeline_mode=pl.Buffered(1)),
                  pl.BlockSpec((1, n), lambda b, i: (0, 0))],
        out_specs=pl.BlockSpec((1, tm, n), lambda b, i: (b, i, 0)),
        out_shape=jax.ShapeDtypeStruct((bsz, t, n), BF16),
        compiler_params=pltpu.CompilerParams(dimension_semantics=("parallel", "arbitrary"),
                                             vmem_limit_bytes=VMEM_LIMIT),
        name="in_proj",
    )(x, mod6, w_in_bf, col_scale)


def _attn_a_tables(tq, tk):
    r = tk // tq
    rel = (np.arange(tq)[:, None] - np.arange(tk)[None, :]).astype(np.float64)
    tabs = []
    for s in SLOPES_A:
        v = [-s * np.abs(rel + c * tq) for c in range(r)]
        v.append(-s * rel)
        v.append(s * rel)
        tabs.append(np.stack(v))
    return jnp.asarray(np.stack(tabs).astype(np.float32))


def _attn_a_kernel(lam_ref, g_ref, tab_ref, q_ref, k_ref, v_ref, o_ref, m_sc, l_sc, acc_sc,
                   *, tq, tk, seq, lambda_init):
    qi = pl.program_id(1)
    r = tk // tq
    nkv = seq // tk
    kd = qi // r
    var_d = qi % r
    i0 = qi * tq
    lane = _lane_iota((1, LANES))
    lf = lam_ref[...]
    lam_full = (jnp.exp(jnp.sum(lf[0:1] * lf[1:2], axis=(0, 1), keepdims=True))
                - jnp.exp(jnp.sum(lf[2:3] * lf[3:4], axis=(0, 1), keepdims=True)) + lambda_init)
    ones11 = jnp.ones((1, 1), jnp.int32)

    for p in range(A_HEADS // 2):
        c0 = p * LANES
        qp = q_ref[0, :, c0:c0 + LANES]
        d_heads = []
        for a in range(2):
            h = 2 * p + a
            slope = SLOPES_A[h]
            o_m = []
            for m2 in range(2):
                lo = a * HEAD_DIM + m2 * A_QK_DIM
                qm = jnp.where((lane >= lo) & (lane < lo + A_QK_DIM), qp, jnp.zeros_like(qp))
                m_sc[...] = jnp.full(m_sc.shape, NEG, F32)
                l_sc[...] = jnp.zeros(l_sc.shape, F32)
                acc_sc[...] = jnp.zeros(acc_sc.shape, F32)

                def body(kj, carry, qm=qm, h=h, slope=slope, c0=c0):
                    j0 = pl.multiple_of(kj * tk, tk)
                    kb = k_ref[0, pl.ds(j0, tk), c0:c0 + LANES]
                    vb = v_ref[0, pl.ds(j0, tk), c0:c0 + LANES]
                    s = _nt_dot(qm, kb)
                    var = jnp.where(kj < kd, r, jnp.where(kj > kd, r + 1, var_d))
                    gap = jnp.where(kj == kd, 0, jnp.abs(i0 - j0))
                    off = (gap * ones11).astype(F32) * (-slope)
                    t = s + tab_ref[h, var]
                    rm = jnp.max(t, axis=1, keepdims=True)
                    m_old = m_sc[...]
                    m_new = jnp.maximum(m_old, rm + off)
                    alpha = jnp.exp(m_old - m_new)
                    pr = jnp.exp(t - (m_new - off))
                    ps = pr[:, 0:LANES]
                    for c in range(1, tk // LANES):
                        ps = ps + pr[:, c * LANES:(c + 1) * LANES]
                    l_sc[...] = alpha * l_sc[...] + ps
                    acc_sc[...] = alpha * acc_sc[...] + jnp.dot(pr.astype(BF16), vb,
                                                                preferred_element_type=F32)
                    m_sc[...] = m_new
                    return carry

                lax.fori_loop(0, nkv, body, 0)
                l_tot = jnp.sum(l_sc[...], axis=1, keepdims=True)
                o_m.append(acc_sc[...] / l_tot)
            d_heads.append(o_m[0] - lam_full * o_m[1])
        first = lane < HEAD_DIM
        dp = jnp.where(first, d_heads[0], d_heads[1])
        sq = dp * dp
        ms0 = jnp.sum(jnp.where(first, sq, 0.0), axis=1, keepdims=True) * (1.0 / HEAD_DIM)
        ms1 = jnp.sum(jnp.where(first, 0.0, sq), axis=1, keepdims=True) * (1.0 / HEAD_DIM)
        ms = jnp.where(first, ms0, ms1)
        out = dp * lax.rsqrt(ms + LN_EPS) * g_ref[...] * (1.0 - lambda_init)
        o_ref[0, :, c0:c0 + LANES] = out.astype(o_ref.dtype)


def _attention_a(proj, lam_l, g2, lambda_init, tq=128, tk=512):
    bsz, t, _ = proj.shape
    tab = _attn_a_tables(tq, tk)
    nvar = tab.shape[1]
    kern = functools.partial(_attn_a_kernel, tq=tq, tk=tk, seq=t, lambda_init=lambda_init)
    return pl.pallas_call(
        kern,
        grid=(bsz, t // tq),
        in_specs=[pl.BlockSpec((4, A_QK_DIM), lambda b, i: (0, 0)),
                  pl.BlockSpec((1, LANES), lambda b, i: (0, 0)),
                  pl.BlockSpec((A_HEADS, nvar, tq, tk), lambda b, i: (0, 0, 0, 0),
                               pipeline_mode=pl.Buffered(1)),
                  pl.BlockSpec((1, tq, A_W), lambda b, i: (b, i, 0)),
                  pl.BlockSpec((1, t, A_W), lambda b, i: (b, 0, 1)),
                  pl.BlockSpec((1, t, A_W), lambda b, i: (b, 0, 2))],
        out_specs=pl.BlockSpec((1, tq, A_W), lambda b, i: (b, i, 0)),
        out_shape=jax.ShapeDtypeStruct((bsz, t, A_W), BF16),
        scratch_shapes=[pltpu.VMEM((tq, 1), F32), pltpu.VMEM((tq, LANES), F32),
                        pltpu.VMEM((tq, LANES), F32)],
        compiler_params=pltpu.CompilerParams(dimension_semantics=("parallel", "arbitrary"),
                                             vmem_limit_bytes=VMEM_LIMIT),
        name="attn_diff",
    )(lam_l, g2, tab, proj, proj, proj)


def _attn_b_kernel(sl_ref, q_ref, k_ref, v_ref, o_ref, qf, kf, vf, m_st, l_st, o_st, *, seq, tq, win):
    lane = _lane_iota((1, LANES))
    first = lane < HEAD_DIM
    chunk = 512

    def prep(c, carry):
        r0 = pl.multiple_of(c * chunk, chunk)
        rows = pl.ds(r0, chunk)
        qf[rows, :] = q_ref[0, rows, :].astype(F32)
        kf[rows, :] = k_ref[0, rows, :].astype(F32)
        vf[rows, :] = v_ref[0, rows, :].astype(F32)
        m_st[rows, :] = jnp.full((chunk, LANES), NEG, F32)
        l_st[rows, :] = jnp.zeros((chunk, LANES), F32)
        o_st[rows, :] = jnp.zeros((chunk, LANES), F32)
        return carry

    lax.fori_loop(0, seq // chunk, prep, 0)

    relb = (lax.broadcasted_iota(jnp.int32, (tq, win), 0) - lax.broadcasted_iota(jnp.int32, (tq, win), 1))

    for d in B_DILATIONS:
        length = seq // d
        nblk = length // tq

        def rows_of(start, size, d=d):
            return pl.ds(start, size) if d == 1 else pl.ds(start, size, stride=d)

        def body(idx, carry, d=d, length=length, nblk=nblk, rows_of=rows_of):
            res = idx // nblk
            blk = idx % nblk
            j0 = blk * tq
            start = jnp.clip(j0 - B_HALF, 0, length - win)
            rq = rows_of(res + j0 * d, tq)
            rk = rows_of(res + start * d, win)
            q = qf[rq, :]
            k = kf[rk, :].astype(BF16)
            v = vf[rk, :].astype(BF16)
            adist = jnp.abs(relb + (j0 - start))
            valid = adist <= B_HALF
            adf = adist.astype(F32)
            m_prev = m_st[rq, :]
            mn, lb, pv = [], [], []
            for a in range(2):
                half = first if a == 0 else jnp.logical_not(first)
                qm = jnp.where(half, q, 0.0).astype(BF16)
                s = _nt_dot(qm, k)
                sl = sl_ref[0, a:a + 1, 0:1] * float(d)
                t = jnp.where(valid, s - sl * adf, NEG)
                mb = jnp.max(t, axis=1, keepdims=True)
                mp = jnp.max(jnp.where(half, m_prev, NEG), axis=1, keepdims=True)
                m_new = jnp.maximum(mp, mb)
                pr = jnp.exp(t - m_new)
                mn.append(m_new)
                lb.append(jnp.sum(pr, axis=1, keepdims=True))
                pv.append(jnp.dot(pr.astype(BF16), v, preferred_element_type=F32))
            m_pair = jnp.where(first, mn[0], mn[1])
            alpha = jnp.exp(m_prev - m_pair)
            l_st[rq, :] = alpha * l_st[rq, :] + jnp.where(first, lb[0], lb[1])
            o_st[rq, :] = alpha * o_st[rq, :] + jnp.where(first, pv[0], pv[1])
            m_st[rq, :] = m_pair
            return carry

        lax.fori_loop(0, seq // tq, body, 0)

    def fin(c, carry):
        r0 = pl.multiple_of(c * chunk, chunk)
        rows = pl.ds(r0, chunk)
        o_ref[0, rows, :] = (o_st[rows, :] / l_st[rows, :]).astype(o_ref.dtype)
        return carry

    lax.fori_loop(0, seq // chunk, fin, 0)


def _attention_b(proj, tq=128, win=256):
    bsz, t, _ = proj.shape
    npair = B_HEADS // 2
    sl = np.zeros((npair, 8, LANES), np.float32)
    for h, s in enumerate(SLOPES_B):
        sl[h // 2, h % 2, :] = s
    qb0 = 3 * A_W // LANES
    kern = functools.partial(_attn_b_kernel, seq=t, tq=tq, win=win)
    seq_spec = lambda off: pl.BlockSpec((1, t, LANES), lambda b, p, off=off: (b, 0, off + p))
    return pl.pallas_call(
        kern,
        grid=(bsz, npair),
        in_specs=[pl.BlockSpec((1, 8, LANES), lambda b, p: (p, 0, 0)),
                  seq_spec(qb0), seq_spec(qb0 + B_W // LANES), seq_spec(qb0 + 2 * B_W // LANES)],
        out_specs=pl.BlockSpec((1, t, LANES), lambda b, p: (b, 0, p)),
        out_shape=jax.ShapeDtypeStruct((bsz, t, B_W), BF16),
        scratch_shapes=[pltpu.VMEM((t, LANES), F32) for _ in range(6)],
        compiler_params=pltpu.CompilerParams(dimension_semantics=("parallel", "arbitrary"),
                                             vmem_limit_bytes=VMEM_LIMIT),
        name="attn_dilated",
    )(jnp.asarray(sl), proj, proj, proj)


def _attn_c_kernel(sink_ref, q_ref, k_ref, v_ref, o_ref, *, seq, tq, win):
    qi = pl.program_id(1)
    i0 = qi * tq
    start = pl.multiple_of(jnp.clip(i0 - C_HALF_WINDOW, 0, seq - win), C_HALF_WINDOW)
    k2 = k_ref[0, pl.ds(start, win), :].astype(F32)
    v2 = v_ref[0, pl.ds(start, win), :]
    k_same = k2.astype(BF16)
    k_swap = pltpu.roll(k2, HEAD_DIM, 1).astype(BF16)
    rel = (lax.broadcasted_iota(jnp.int32, (tq, win), 0) - lax.broadcasted_iota(jnp.int32, (tq, win), 1)
           + (i0 - start))
    dist = jnp.abs(rel)
    valid = dist <= C_HALF_WINDOW
    distf = dist.astype(F32)
    lane = _lane_iota((1, LANES))
    first = lane < HEAD_DIM
    rep = C_HEADS // C_KV_HEADS
    for p in range(C_HEADS // 2):
        c0 = p * LANES
        qp = q_ref[0, :, c0:c0 + LANES]
        ms, ls, pvs = [], [], []
        for a in range(2):
            h = 2 * p + a
            g = h // rep
            half = first if a == 0 else jnp.logical_not(first)
            qm = jnp.where(half, qp, jnp.zeros_like(qp))
            s = _nt_dot(qm, k_same if a == g else k_swap)
            t = jnp.where(valid, s - SLOPES_C[h] * distf, NEG)
            m = jnp.max(t, axis=1, keepdims=True)
            pr = jnp.exp(t - m)
            ms.append(m)
            ls.append(jnp.sum(pr, axis=1, keepdims=True))
            pv = jnp.dot(pr.astype(BF16), v2, preferred_element_type=F32)
            pvs.append(pv if a == g else pltpu.roll(pv, HEAD_DIM, 1))
        m_pair = jnp.where(first, ms[0], ms[1])
        l_pair = jnp.where(first, ls[0], ls[1])
        o_pair = jnp.where(first, pvs[0], pvs[1])
        sk = sink_ref[:, c0:c0 + LANES]
        mx = jnp.maximum(m_pair, sk)
        a_ = jnp.exp(m_pair - mx)
        out = o_pair * a_ / (l_pair * a_ + jnp.exp(sk - mx))
        o_ref[0, :, c0:c0 + LANES] = out.astype(o_ref.dtype)


def _attention_c(proj, sink_lanes, tq=128, win=384):
    bsz, t, _ = proj.shape
    kern = functools.partial(_attn_c_kernel, seq=t, tq=tq, win=win)
    qc0 = (3 * A_W + 3 * B_W) // C_QW
    kc0 = (3 * A_W + 3 * B_W + C_QW) // C_KW
    return pl.pallas_call(
        kern,
        grid=(bsz, t // tq),
        in_specs=[pl.BlockSpec((1, C_QW), lambda b, i: (0, 0)),
                  pl.BlockSpec((1, tq, C_QW), lambda b, i: (b, i, qc0)),
                  pl.BlockSpec((1, t, C_KW), lambda b, i: (b, 0, kc0)),
                  pl.BlockSpec((1, t, C_KW), lambda b, i: (b, 0, kc0 + 1))],
        out_specs=pl.BlockSpec((1, tq, C_QW), lambda b, i: (b, i, 0)),
        out_shape=jax.ShapeDtypeStruct((bsz, t, C_QW), BF16),
        compiler_params=pltpu.CompilerParams(dimension_semantics=("parallel", "arbitrary"),
                                             vmem_limit_bytes=VMEM_LIMIT),
        name="attn_window",
    )(sink_lanes, proj, proj, proj)


def _layer_norm(y, g, b):
    mu = jnp.mean(y, axis=-1, keepdims=True)
    yc = y - mu
    var = jnp.mean(yc * yc, axis=-1, keepdims=True)
    return yc * lax.rsqrt(var + LN_EPS) * g + b


def _tail_kernel(oa_ref, ob_ref, oc_ref, x_ref, mod_ref, wo_ref, wgu_ref, wd_ref, lng_ref, lnb_ref, o_ref,
                 *, alpha, hidden, ck):
    mix = (jnp.dot(oa_ref[0], wo_ref[0:A_W, :], preferred_element_type=F32)
           + jnp.dot(ob_ref[0], wo_ref[A_W:A_W + B_W, :], preferred_element_type=F32)
           + jnp.dot(oc_ref[0], wo_ref[A_W + B_W:, :], preferred_element_type=F32))
    g1 = mod_ref[0, 2:3, :]
    x1 = _layer_norm(alpha * x_ref[0] + (1.0 + g1) * mix, lng_ref[0:1, :], lnb_ref[0:1, :])
    sh2 = mod_ref[0, 3:4, :]
    sc2 = mod_ref[0, 4:5, :]
    g2 = mod_ref[0, 5:6, :]
    h2 = (x1 * (1.0 + sc2) + sh2).astype(BF16)
    acc = jnp.zeros(x1.shape, F32)
    for c0 in range(0, hidden, ck):
        gate = jnp.dot(h2, wgu_ref[:, c0:c0 + ck], preferred_element_type=F32)
        up = jnp.dot(h2, wgu_ref[:, hidden + c0:hidden + c0 + ck], preferred_element_type=F32)
        act = (gate * jax.nn.sigmoid(gate) * up).astype(BF16)
        acc = acc + jnp.dot(act, wd_ref[c0:c0 + ck, :], preferred_element_type=F32)
    o_ref[0] = _layer_norm(alpha * x1 + (1.0 + g2) * acc, lng_ref[1:2, :], lnb_ref[1:2, :])


def _tail(oa, ob, oc, x, mod6, wo_bf, wgu_bf, wd_bf, ln_g, ln_b, alpha, tm=512):
    bsz, t, d = x.shape
    hidden = wd_bf.shape[0]
    kern = functools.partial(_tail_kernel, alpha=alpha, hidden=hidden, ck=256)
    row = lambda w: pl.BlockSpec((1, tm, w), lambda b, i: (b, i, 0))
    whole = lambda shp: pl.BlockSpec(shp, lambda b, i: (0, 0), pipeline_mode=pl.Buffered(1))
    return pl.pallas_call(
        kern,
        grid=(bsz, t // tm),
        in_specs=[row(A_W), row(B_W), row(C_QW), row(d),
                  pl.BlockSpec((1, 6, d), lambda b, i: (b, 0, 0)),
                  whole(wo_bf.shape), whole(wgu_bf.shape), whole(wd_bf.shape),
                  pl.BlockSpec((2, d), lambda b, i: (0, 0)),
                  pl.BlockSpec((2, d), lambda b, i: (0, 0))],
        out_specs=row(d),
        out_shape=jax.ShapeDtypeStruct((bsz, t, d), F32),
        compiler_params=pltpu.CompilerParams(dimension_semantics=("parallel", "arbitrary"),
                                             vmem_limit_bytes=VMEM_LIMIT),
        name="outproj_ffn",
    )(oa, ob, oc, x, mod6, wo_bf, wgu_bf, wd_bf, ln_g, ln_b)


def _q_col_scale():
    cs = np.ones((1, IN_WIDTH), np.float32)
    cs[0, 0:A_W] = A_QK_DIM ** -0.5
    cs[0, 3 * A_W:3 * A_W + B_W] = HEAD_DIM ** -0.5
    cs[0, 3 * A_W + 3 * B_W:3 * A_W + 3 * B_W + C_QW] = HEAD_DIM ** -0.5
    return jnp.asarray(cs)


def kernel(x, c, w_ada, b_ada, w_in, lam, subln_g, sink, w_out, ln_g, ln_b, w_gu, w_down):
    depth = w_ada.shape[0]
    bsz, _, d = x.shape
    alpha = (2 * depth) ** 0.25
    mod = _modulation(c, w_ada, b_ada).reshape(depth, bsz, 6, d)
    col_scale = _q_col_scale()
    for layer in range(depth):
        lambda_init = 0.8 - 0.6 * math.exp(-0.3 * layer)
        proj = _in_projection(x, mod[layer], w_in[layer].astype(BF16), col_scale)
        g2 = jnp.tile(subln_g[layer], 2).reshape(1, LANES)
        oa = _attention_a(proj, lam[layer], g2, lambda_init)
        ob = _attention_b(proj)
        oc = _attention_c(proj, jnp.repeat(sink[layer], HEAD_DIM).reshape(1, C_QW))
        x = _tail(oa, ob, oc, x, mod[layer], w_out[layer].astype(BF16), w_gu[layer].astype(BF16),
                  w_down[layer].astype(BF16), ln_g[layer], ln_b[layer], alpha)
    return x
```

```python
import functools
import math

import numpy as np
import jax
import jax.numpy as jnp
from jax import lax
from jax.experimental import pallas as pl
from jax.experimental.pallas import tpu as pltpu

F32 = jnp.float32
BF16 = jnp.bfloat16

HEAD_DIM = 64
A_HEADS = 4
A_QK_DIM = 32
B_HEADS = 6
B_DILATIONS = (1, 4, 16)
B_HALF = 64
C_HEADS = 6
C_KV_HEADS = 2
C_HALF_WINDOW = 128
N_ATTN_HEADS = 16
LN_EPS = 1e-5
NEG = -1e30
LOG2E = math.log2(math.e)

LANES = 128
A_W = A_HEADS * HEAD_DIM
B_W = B_HEADS * HEAD_DIM
C_QW = C_HEADS * HEAD_DIM
C_KW = C_KV_HEADS * HEAD_DIM
IN_WIDTH = 3 * A_W + 3 * B_W + C_QW + 2 * C_KW
VMEM_LIMIT = 56 * 1024 * 1024

_S = 2.0 ** (-8.0 * np.arange(1, N_ATTN_HEADS + 1) / N_ATTN_HEADS)
SLOPES_C = [float(v) for v in _S[:C_HEADS]]
SLOPES_A = [float(v) for v in _S[C_HEADS:C_HEADS + A_HEADS]]
SLOPES_B = [float(v) for v in _S[C_HEADS + A_HEADS:]]


def _nt_dot(a, b):
    return lax.dot_general(a, b, (((1,), (1,)), ((), ())), preferred_element_type=F32)


def _lane_iota(shape):
    return lax.broadcasted_iota(jnp.int32, shape, len(shape) - 1)


def _mod_kernel(c_ref, w_ref, b_ref, o_ref):
    c = c_ref[...]
    a = (c * jax.nn.sigmoid(c)).astype(BF16)
    o_ref[0] = jnp.dot(a, w_ref[0].astype(BF16), preferred_element_type=F32) + b_ref[0]


def _modulation(c, w_ada, b_ada):
    depth, d, n = w_ada.shape
    bsz = c.shape[0]
    tn = 1536
    return pl.pallas_call(
        _mod_kernel,
        grid=(depth, n // tn),
        in_specs=[pl.BlockSpec((bsz, d), lambda l, j: (0, 0)),
                  pl.BlockSpec((1, d, tn), lambda l, j: (l, 0, j)),
                  pl.BlockSpec((1, 1, tn), lambda l, j: (l, 0, j))],
        out_specs=pl.BlockSpec((1, bsz, tn), lambda l, j: (l, 0, j)),
        out_shape=jax.ShapeDtypeStruct((depth, bsz, n), F32),
        compiler_params=pltpu.CompilerParams(dimension_semantics=("arbitrary", "arbitrary"),
                                             vmem_limit_bytes=VMEM_LIMIT),
        name="adaln_mod",
    )(c, w_ada, b_ada.reshape(depth, 1, n))


def _proj_kernel(x_ref, mod_ref, w_ref, cs_ref, o_ref, *, tn):
    sh = mod_ref[0, 0:1, :]
    sc = mod_ref[0, 1:2, :]
    h = (x_ref[0] * (1.0 + sc) + sh).astype(BF16)
    for n0 in range(0, o_ref.shape[2], tn):
        acc = jnp.dot(h, w_ref[:, n0:n0 + tn], preferred_element_type=F32)
        o_ref[0, :, n0:n0 + tn] = (acc * cs_ref[:, n0:n0 + tn]).astype(BF16)


def _in_projection(x, mod6, w_in_bf, col_scale, tm=512):
    bsz, t, d = x.shape
    n = w_in_bf.shape[1]
    return pl.pallas_call(
        functools.partial(_proj_kernel, tn=512),
        grid=(bsz, t // tm),
        in_specs=[pl.BlockSpec((1, tm, d), lambda b, i: (b, i, 0)),
                  pl.BlockSpec((1, 6, d), lambda b, i: (b, 0, 0)),
                  pl.BlockSpec((d, n), lambda b, i: (0, 0), pipeline_mode=pl.Buffered(1)),
                  pl.BlockSpec((1, n), lambda b, i: (0, 0))],
        out_specs=pl.BlockSpec((1, tm, n), lambda b, i: (b, i, 0)),
        out_shape=jax.ShapeDtypeStruct((bsz, t, n), BF16),
        compiler_params=pltpu.CompilerParams(dimension_semantics=("parallel", "arbitrary"),
                                             vmem_limit_bytes=VMEM_LIMIT),
        name="in_proj",
    )(x, mod6, w_in_bf, col_scale)


def _attn_a_tables(tq, tk):
    r = tk // tq
    rel = (np.arange(tq)[:, None] - np.arange(tk)[None, :]).astype(np.float64)
    tabs = []
    for s in SLOPES_A:
        s2 = s * LOG2E
        v = [-s2 * np.abs(rel + c * tq) for c in range(r)]
        v.append(-s2 * rel)
        v.append(s2 * rel)
        tabs.append(np.stack(v))
    return jnp.asarray(np.stack(tabs).astype(np.float32))


def _attn_a_kernel(lam_ref, g_ref, sl_ref, tab_ref, q_ref, k_ref, v_ref, o_ref, q4_sc, t_sc, acc_sc,
                   *, tq, tk, seq, lambda_init):
    qi = pl.program_id(1)
    r = tk // tq
    nkv = seq // tk
    kd = qi // r
    var_d = qi % r
    i0 = qi * tq
    lane = _lane_iota((1, LANES))
    first = lane < HEAD_DIM
    npair = A_HEADS // 2
    ones11 = jnp.ones((1, 1), jnp.int32)

    for p in range(npair):
        qp = q_ref[0, :, p * LANES:(p + 1) * LANES]
        for g in range(4):
            lo = g * A_QK_DIM
            q4_sc[p, g] = jnp.where((lane >= lo) & (lane < lo + A_QK_DIM), qp, jnp.zeros_like(qp))

    variants = [jnp.where(c < kd, r, jnp.where(c > kd, r + 1, var_d)) for c in range(nkv)]
    gaps = [(jnp.where(c == kd, 0, jnp.abs(i0 - c * tk)) * ones11).astype(F32) for c in range(nkv)]

    for p in range(npair):
        cols = slice(p * LANES, (p + 1) * LANES)

        def map_body(g, carry, p=p, cols=cols):
            a = g // 2
            h = 2 * p + a
            qg = q4_sc[p, g]
            slope2 = sl_ref[p, pl.ds(g, 1), 0:1]
            offs = [gap * (-slope2) for gap in gaps]
            mx = None
            for c in range(nkv):
                s = _nt_dot(qg, k_ref[0, c * tk:(c + 1) * tk, cols])
                t = s + tab_ref[h, variants[c]]
                t_sc[:, c * tk:(c + 1) * tk] = t
                cm = t[:, 0:LANES]
                for lt in range(1, tk // LANES):
                    cm = jnp.maximum(cm, t[:, lt * LANES:(lt + 1) * LANES])
                cm = cm + offs[c]
                mx = cm if mx is None else jnp.maximum(mx, cm)
            m = jnp.max(mx, axis=1, keepdims=True)
            mb = jnp.broadcast_to(m, (tq, LANES))
            keep = (lane >= a * HEAD_DIM) & (lane < (a + 1) * HEAD_DIM)
            acc = jnp.zeros((tq, LANES), F32)
            for c in range(nkv):
                mbc = mb - offs[c]
                pr = [jnp.exp2(t_sc[:, c * tk + lt * LANES:c * tk + (lt + 1) * LANES] - mbc).astype(BF16)
                      for lt in range(tk // LANES)]
                vb = v_ref[0, c * tk:(c + 1) * tk, cols]
                v_aug = jnp.where(keep, vb, jnp.ones_like(vb))
                acc = acc + jnp.dot(jnp.concatenate(pr, axis=1), v_aug, preferred_element_type=F32)
            acc_sc[4 * p + g] = acc
            return carry

        lax.fori_loop(0, 4, map_body, 0)

    lf = lam_ref[...]
    lam_full = (jnp.exp(jnp.sum(lf[0:1] * lf[1:2], axis=(0, 1), keepdims=True))
                - jnp.exp(jnp.sum(lf[2:3] * lf[3:4], axis=(0, 1), keepdims=True)) + lambda_init)
    for p in range(npair):
        d_heads = []
        for a in range(2):
            o_m = []
            for m2 in range(2):
                acc = acc_sc[4 * p + 2 * a + m2]
                o_m.append(acc / pltpu.roll(acc, HEAD_DIM, 1))
            d_heads.append(o_m[0] - lam_full * o_m[1])
        dp = jnp.where(first, d_heads[0], d_heads[1])
        sq = dp * dp
        ms0 = jnp.sum(jnp.where(first, sq, 0.0), axis=1, keepdims=True) * (1.0 / HEAD_DIM)
        ms1 = jnp.sum(jnp.where(first, 0.0, sq), axis=1, keepdims=True) * (1.0 / HEAD_DIM)
        ms = jnp.where(first, ms0, ms1)
        out = dp * lax.rsqrt(ms + LN_EPS) * g_ref[...] * (1.0 - lambda_init)
        o_ref[0, :, p * LANES:(p + 1) * LANES] = out.astype(o_ref.dtype)


def _attention_a(proj, lam_l, g2, lambda_init, tq=256, tk=512):
    bsz, t, _ = proj.shape
    tab = _attn_a_tables(tq, tk)
    nvar = tab.shape[1]
    npair = A_HEADS // 2
    sl = np.zeros((npair, 8, LANES), np.float32)
    for p in range(npair):
        for g in range(4):
            sl[p, g, :] = SLOPES_A[2 * p + g // 2] * LOG2E
    kern = functools.partial(_attn_a_kernel, tq=tq, tk=tk, seq=t, lambda_init=lambda_init)
    return pl.pallas_call(
        kern,
        grid=(bsz, t // tq),
        in_specs=[pl.BlockSpec((4, A_QK_DIM), lambda b, i: (0, 0)),
                  pl.BlockSpec((1, LANES), lambda b, i: (0, 0)),
                  pl.BlockSpec((npair, 8, LANES), lambda b, i: (0, 0, 0)),
                  pl.BlockSpec((A_HEADS, nvar, tq, tk), lambda b, i: (0, 0, 0, 0),
                               pipeline_mode=pl.Buffered(1)),
                  pl.BlockSpec((1, tq, A_W), lambda b, i: (b, i, 0)),
                  pl.BlockSpec((1, t, A_W), lambda b, i: (b, 0, 1)),
                  pl.BlockSpec((1, t, A_W), lambda b, i: (b, 0, 2))],
        out_specs=pl.BlockSpec((1, tq, A_W), lambda b, i: (b, i, 0)),
        out_shape=jax.ShapeDtypeStruct((bsz, t, A_W), BF16),
        scratch_shapes=[pltpu.VMEM((npair, 4, tq, LANES), BF16),
                        pltpu.VMEM((tq, t), F32),
                        pltpu.VMEM((2 * A_HEADS, tq, LANES), F32)],
        compiler_params=pltpu.CompilerParams(dimension_semantics=("parallel", "arbitrary"),
                                             vmem_limit_bytes=VMEM_LIMIT),
        name="attn_diff",
    )(lam_l, g2, jnp.asarray(sl), tab, proj, proj, proj)


def _attn_b_kernel(sl_ref, q_ref, k_ref, v_ref, o_ref, qf, kf, vf, m_st, l_st, o_st, *, seq, tq, win):
    lane = _lane_iota((1, LANES))
    first = lane < HEAD_DIM
    chunk = 512

    def prep(c, carry):
        r0 = pl.multiple_of(c * chunk, chunk)
        rows = pl.ds(r0, chunk)
        qf[rows, :] = q_ref[0, rows, :].astype(F32)
        kf[rows, :] = k_ref[0, rows, :].astype(F32)
        vf[rows, :] = v_ref[0, rows, :].astype(F32)
        m_st[rows, :] = jnp.full((chunk, LANES), NEG, F32)
        l_st[rows, :] = jnp.zeros((chunk, LANES), F32)
        o_st[rows, :] = jnp.zeros((chunk, LANES), F32)
        return carry

    lax.fori_loop(0, seq // chunk, prep, 0)

    relb = (lax.broadcasted_iota(jnp.int32, (tq, win), 0) - lax.broadcasted_iota(jnp.int32, (tq, win), 1))

    for d in B_DILATIONS:
        length = seq // d
        nblk = length // tq

        def rows_of(start, size, d=d):
            return pl.ds(start, size) if d == 1 else pl.ds(start, size, stride=d)

        def body(idx, carry, d=d, length=length, nblk=nblk, rows_of=rows_of):
            res = idx // nblk
            blk = idx % nblk
            j0 = blk * tq
            start = jnp.clip(j0 - B_HALF, 0, length - win)
            rq = rows_of(res + j0 * d, tq)
            rk = rows_of(res + start * d, win)
            q = qf[rq, :]
            k = kf[rk, :].astype(BF16)
            v = vf[rk, :].astype(BF16)
            adist = jnp.abs(relb + (j0 - start))
            valid = adist <= B_HALF
            adf = adist.astype(F32)
            m_prev = m_st[rq, :]
            mn, lb, pv = [], [], []
            for a in range(2):
                half = first if a == 0 else jnp.logical_not(first)
                qm = jnp.where(half, q, 0.0).astype(BF16)
                s = _nt_dot(qm, k)
                sl = sl_ref[0, a:a + 1, 0:1] * float(d)
                t = jnp.where(valid, s - sl * adf, NEG)
                mb = jnp.max(t, axis=1, keepdims=True)
                mp = jnp.max(jnp.where(half, m_prev, NEG), axis=1, keepdims=True)
                m_new = jnp.maximum(mp, mb)
                pr = jnp.exp2(t - m_new)
                mn.append(m_new)
                lb.append(jnp.sum(pr, axis=1, keepdims=True))
                pv.append(jnp.dot(pr.astype(BF16), v, preferred_element_type=F32))
            m_pair = jnp.where(first, mn[0], mn[1])
            alpha = jnp.exp2(m_prev - m_pair)
            l_st[rq, :] = alpha * l_st[rq, :] + jnp.where(first, lb[0], lb[1])
            o_st[rq, :] = alpha * o_st[rq, :] + jnp.where(first, pv[0], pv[1])
            m_st[rq, :] = m_pair
            return carry

        lax.fori_loop(0, seq // tq, body, 0)

    def fin(c, carry):
        r0 = pl.multiple_of(c * chunk, chunk)
        rows = pl.ds(r0, chunk)
        o_ref[0, rows, :] = (o_st[rows, :] / l_st[rows, :]).astype(o_ref.dtype)
        return carry

    lax.fori_loop(0, seq // chunk, fin, 0)


def _attention_b(proj, tq=128, win=256):
    bsz, t, _ = proj.shape
    npair = B_HEADS // 2
    sl = np.zeros((npair, 8, LANES), np.float32)
    for h, s in enumerate(SLOPES_B):
        sl[h // 2, h % 2, :] = s * LOG2E
    qb0 = 3 * A_W // LANES
    kern = functools.partial(_attn_b_kernel, seq=t, tq=tq, win=win)
    seq_spec = lambda off: pl.BlockSpec((1, t, LANES), lambda b, p, off=off: (b, 0, off + p))
    return pl.pallas_call(
        kern,
        grid=(bsz, npair),
        in_specs=[pl.BlockSpec((1, 8, LANES), lambda b, p: (p, 0, 0)),
                  seq_spec(qb0), seq_spec(qb0 + B_W // LANES), seq_spec(qb0 + 2 * B_W // LANES)],
        out_specs=pl.BlockSpec((1, t, LANES), lambda b, p: (b, 0, p)),
        out_shape=jax.ShapeDtypeStruct((bsz, t, B_W), BF16),
        scratch_shapes=[pltpu.VMEM((t, LANES), F32) for _ in range(6)],
        compiler_params=pltpu.CompilerParams(dimension_semantics=("parallel", "arbitrary"),
                                             vmem_limit_bytes=VMEM_LIMIT),
        name="attn_dilated",
    )(jnp.asarray(sl), proj, proj, proj)


def _attn_c_kernel(sink_ref, q_ref, k_ref, v_ref, o_ref, *, seq, tq, win):
    qi = pl.program_id(1)
    i0 = qi * tq
    start = pl.multiple_of(jnp.clip(i0 - C_HALF_WINDOW, 0, seq - win), C_HALF_WINDOW)
    k2 = k_ref[0, pl.ds(start, win), :].astype(F32)
    v2 = v_ref[0, pl.ds(start, win), :]
    k_same = k2.astype(BF16)
    k_swap = pltpu.roll(k2, HEAD_DIM, 1).astype(BF16)
    rel = (lax.broadcasted_iota(jnp.int32, (tq, win), 0) - lax.broadcasted_iota(jnp.int32, (tq, win), 1)
           + (i0 - start))
    dist = jnp.abs(rel)
    valid = dist <= C_HALF_WINDOW
    distf = dist.astype(F32)
    lane = _lane_iota((1, LANES))
    first = lane < HEAD_DIM
    rep = C_HEADS // C_KV_HEADS
    for p in range(C_HEADS // 2):
        c0 = p * LANES
        qp = q_ref[0, :, c0:c0 + LANES]
        ms, ls, pvs = [], [], []
        for a in range(2):
            h = 2 * p + a
            g = h // rep
            half = first if a == 0 else jnp.logical_not(first)
            qm = jnp.where(half, qp, jnp.zeros_like(qp))
            s = _nt_dot(qm, k_same if a == g else k_swap)
            t = jnp.where(valid, s - (SLOPES_C[h] * LOG2E) * distf, NEG)
            m = jnp.max(t, axis=1, keepdims=True)
            pr = jnp.exp2(t - m)
            ms.append(m)
            ls.append(jnp.sum(pr, axis=1, keepdims=True))
            pv = jnp.dot(pr.astype(BF16), v2, preferred_element_type=F32)
            pvs.append(pv if a == g else pltpu.roll(pv, HEAD_DIM, 1))
        m_pair = jnp.where(first, ms[0], ms[1])
        l_pair = jnp.where(first, ls[0], ls[1])
        o_pair = jnp.where(first, pvs[0], pvs[1])
        sk = sink_ref[:, c0:c0 + LANES] * LOG2E
        mx = jnp.maximum(m_pair, sk)
        a_ = jnp.exp2(m_pair - mx)
        out = o_pair * a_ / (l_pair * a_ + jnp.exp2(sk - mx))
        o_ref[0, :, c0:c0 + LANES] = out.astype(o_ref.dtype)


def _attention_c(proj, sink_lanes, tq=128, win=384):
    bsz, t, _ = proj.shape
    kern = functools.partial(_attn_c_kernel, seq=t, tq=tq, win=win)
    qc0 = (3 * A_W + 3 * B_W) // C_QW
    kc0 = (3 * A_W + 3 * B_W + C_QW) // C_KW
    return pl.pallas_call(
        kern,
        grid=(bsz, t // tq),
        in_specs=[pl.BlockSpec((1, C_QW), lambda b, i: (0, 0)),
                  pl.BlockSpec((1, tq, C_QW), lambda b, i: (b, i, qc0)),
                  pl.BlockSpec((1, t, C_KW), lambda b, i: (b, 0, kc0)),
                  pl.BlockSpec((1, t, C_KW), lambda b, i: (b, 0, kc0 + 1))],
        out_specs=pl.BlockSpec((1, tq, C_QW), lambda b, i: (b, i, 0)),
        out_shape=jax.ShapeDtypeStruct((bsz, t, C_QW), BF16),
        compiler_params=pltpu.CompilerParams(dimension_semantics=("parallel", "arbitrary"),
                                             vmem_limit_bytes=VMEM_LIMIT),
        name="attn_window",
    )(sink_lanes, proj, proj, proj)


def _layer_norm(y, g, b):
    mu = jnp.mean(y, axis=-1, keepdims=True)
    yc = y - mu
    var = jnp.mean(yc * yc, axis=-1, keepdims=True)
    return yc * lax.rsqrt(var + LN_EPS) * g + b


def _tail_kernel(oa_ref, ob_ref, oc_ref, x_ref, mod_ref, wo_ref, wgu_ref, wd_ref, lng_ref, lnb_ref, o_ref,
                 *, alpha, hidden, ck):
    mix = (jnp.dot(oa_ref[0], wo_ref[0:A_W, :], preferred_element_type=F32)
           + jnp.dot(ob_ref[0], wo_ref[A_W:A_W + B_W, :], preferred_element_type=F32)
           + jnp.dot(oc_ref[0], wo_ref[A_W + B_W:, :], preferred_element_type=F32))
    g1 = mod_ref[0, 2:3, :]
    x1 = _layer_norm(alpha * x_ref[0] + (1.0 + g1) * mix, lng_ref[0:1, :], lnb_ref[0:1, :])
    sh2 = mod_ref[0, 3:4, :]
    sc2 = mod_ref[0, 4:5, :]
    g2 = mod_ref[0, 5:6, :]
    h2 = (x1 * (1.0 + sc2) + sh2).astype(BF16)
    acc = jnp.zeros(x1.shape, F32)
    for c0 in range(0, hidden, ck):
        gate = jnp.dot(h2, wgu_ref[:, c0:c0 + ck], preferred_element_type=F32)
        up = jnp.dot(h2, wgu_ref[:, hidden + c0:hidden + c0 + ck], preferred_element_type=F32)
        act = (gate * jax.nn.sigmoid(gate) * up).astype(BF16)
        acc = acc + jnp.dot(act, wd_ref[c0:c0 + ck, :], preferred_element_type=F32)
    o_ref[0] = _layer_norm(alpha * x1 + (1.0 + g2) * acc, lng_ref[1:2, :], lnb_ref[1:2, :])


def _tail(oa, ob, oc, x, mod6, wo_bf, wgu_bf, wd_bf, ln_g, ln_b, alpha, tm=512):
    bsz, t, d = x.shape
    hidden = wd_bf.shape[0]
    kern = functools.partial(_tail_kernel, alpha=alpha, hidden=hidden, ck=256)
    row = lambda w: pl.BlockSpec((1, tm, w), lambda b, i: (b, i, 0))
    whole = lambda shp: pl.BlockSpec(shp, lambda b, i: (0, 0), pipeline_mode=pl.Buffered(1))
    return pl.pallas_call(
        kern,
        grid=(bsz, t // tm),
        in_specs=[row(A_W), row(B_W), row(C_QW), row(d),
                  pl.BlockSpec((1, 6, d), lambda b, i: (b, 0, 0)),
                  whole(wo_bf.shape), whole(wgu_bf.shape), whole(wd_bf.shape),
                  pl.BlockSpec((2, d), lambda b, i: (0, 0)),
                  pl.BlockSpec((2, d), lambda b, i: (0, 0))],
        out_specs=row(d),
        out_shape=jax.ShapeDtypeStruct((bsz, t, d), F32),
        compiler_params=pltpu.CompilerParams(dimension_semantics=("parallel", "arbitrary"),
                                             vmem_limit_bytes=VMEM_LIMIT),
        name="outproj_ffn",
    )(oa, ob, oc, x, mod6, wo_bf, wgu_bf, wd_bf, ln_g, ln_b)


def _q_col_scale():
    cs = np.ones((1, IN_WIDTH), np.float32)
    cs[0, 0:A_W] = A_QK_DIM ** -0.5 * LOG2E
    cs[0, 3 * A_W:3 * A_W + B_W] = HEAD_DIM ** -0.5 * LOG2E
    cs[0, 3 * A_W + 3 * B_W:3 * A_W + 3 * B_W + C_QW] = HEAD_DIM ** -0.5 * LOG2E
    return jnp.asarray(cs)


def kernel(x, c, w_ada, b_ada, w_in, lam, subln_g, sink, w_out, ln_g, ln_b, w_gu, w_down):
    depth = w_ada.shape[0]
    bsz, _, d = x.shape
    alpha = (2 * depth) ** 0.25
    mod = _modulation(c, w_ada, b_ada).reshape(depth, bsz, 6, d)
    col_scale = _q_col_scale()
    for layer in range(depth):
        lambda_init = 0.8 - 0.6 * math.exp(-0.3 * layer)
        proj = _in_projection(x, mod[layer], w_in[layer].astype(BF16), col_scale)
        g2 = jnp.tile(subln_g[layer], 2).reshape(1, LANES)
        oa = _attention_a(proj, lam[layer], g2, lambda_init)
        ob = _attention_b(proj)
        oc = _attention_c(proj, jnp.repeat(sink[layer], HEAD_DIM).reshape(1, C_QW))
        x = _tail(oa, ob, oc, x, mod[layer], w_out[layer].astype(BF16), w_gu[layer].astype(BF16),
                  w_down[layer].astype(BF16), ln_g[layer], ln_b[layer], alpha)
    return x
```

```python
import functools
import math

import numpy as np
import jax
import jax.numpy as jnp
from jax import lax
from jax.experimental import pallas as pl
from jax.experimental.pallas import tpu as pltpu

F32 = jnp.float32
BF16 = jnp.bfloat16

HEAD_DIM = 64
A_HEADS = 4
A_QK_DIM = 32
B_HEADS = 6
B_DILATIONS = (1, 4, 16)
B_HALF = 64
C_HEADS = 6
C_KV_HEADS = 2
C_HALF_WINDOW = 128
N_ATTN_HEADS = 16
LN_EPS = 1e-5
NEG = -1e30
LOG2E = math.log2(math.e)

LANES = 128
A_W = A_HEADS * HEAD_DIM
B_W = B_HEADS * HEAD_DIM
C_QW = C_HEADS * HEAD_DIM
C_KW = C_KV_HEADS * HEAD_DIM
IN_WIDTH = 3 * A_W + 3 * B_W + C_QW + 2 * C_KW
VMEM_LIMIT = 56 * 1024 * 1024

_S = 2.0 ** (-8.0 * np.arange(1, N_ATTN_HEADS + 1) / N_ATTN_HEADS)
SLOPES_C = [float(v) for v in _S[:C_HEADS]]
SLOPES_A = [float(v) for v in _S[C_HEADS:C_HEADS + A_HEADS]]
SLOPES_B = [float(v) for v in _S[C_HEADS + A_HEADS:]]


def _nt_dot(a, b):
    return lax.dot_general(a, b, (((1,), (1,)), ((), ())), preferred_element_type=F32)


def _lane_iota(shape):
    return lax.broadcasted_iota(jnp.int32, shape, len(shape) - 1)


def _mod_kernel(c_ref, w_ref, b_ref, o_ref):
    c = c_ref[...]
    a = (c * jax.nn.sigmoid(c)).astype(BF16)
    o_ref[0] = jnp.dot(a, w_ref[0].astype(BF16), preferred_element_type=F32) + b_ref[0]


def _modulation(c, w_ada, b_ada):
    depth, d, n = w_ada.shape
    bsz = c.shape[0]
    tn = 1536
    return pl.pallas_call(
        _mod_kernel,
        grid=(depth, n // tn),
        in_specs=[pl.BlockSpec((bsz, d), lambda l, j: (0, 0)),
                  pl.BlockSpec((1, d, tn), lambda l, j: (l, 0, j)),
                  pl.BlockSpec((1, 1, tn), lambda l, j: (l, 0, j))],
        out_specs=pl.BlockSpec((1, bsz, tn), lambda l, j: (l, 0, j)),
        out_shape=jax.ShapeDtypeStruct((depth, bsz, n), F32),
        compiler_params=pltpu.CompilerParams(dimension_semantics=("arbitrary", "arbitrary"),
                                             vmem_limit_bytes=VMEM_LIMIT),
        name="adaln_mod",
    )(c, w_ada, b_ada.reshape(depth, 1, n))


def _proj_kernel(x_ref, mod_ref, w_ref, cs_ref, o_ref, *, tn):
    sh = mod_ref[0, 0:1, :]
    sc = mod_ref[0, 1:2, :]
    h = (x_ref[0] * (1.0 + sc) + sh).astype(BF16)
    for n0 in range(0, o_ref.shape[2], tn):
        acc = jnp.dot(h, w_ref[:, n0:n0 + tn], preferred_element_type=F32)
        o_ref[0, :, n0:n0 + tn] = (acc * cs_ref[:, n0:n0 + tn]).astype(BF16)


def _in_projection(x, mod6, w_in_bf, col_scale, tm=512):
    bsz, t, d = x.shape
    n = w_in_bf.shape[1]
    return pl.pallas_call(
        functools.partial(_proj_kernel, tn=512),
        grid=(bsz, t // tm),
        in_specs=[pl.BlockSpec((1, tm, d), lambda b, i: (b, i, 0)),
                  pl.BlockSpec((1, 6, d), lambda b, i: (b, 0, 0)),
                  pl.BlockSpec((d, n), lambda b, i: (0, 0), pipeline_mode=pl.Buffered(1)),
                  pl.BlockSpec((1, n), lambda b, i: (0, 0))],
        out_specs=pl.BlockSpec((1, tm, n), lambda b, i: (b, i, 0)),
        out_shape=jax.ShapeDtypeStruct((bsz, t, n), BF16),
        compiler_params=pltpu.CompilerParams(dimension_semantics=("parallel", "arbitrary"),
                                             vmem_limit_bytes=VMEM_LIMIT),
        name="in_proj",
    )(x, mod6, w_in_bf, col_scale)


def _attn_a_tables(tq, tk):
    r = tk // tq
    rel = (np.arange(tq)[:, None] - np.arange(tk)[None, :]).astype(np.float64)
    tabs = []
    for s in SLOPES_A:
        s2 = s * LOG2E
        v = [-s2 * np.abs(rel + c * tq) for c in range(r)]
        v.append(-s2 * rel)
        v.append(s2 * rel)
        tabs.append(np.stack(v))
    return jnp.asarray(np.stack(tabs).astype(np.float32))


def _attn_a_kernel(lam_ref, g_ref, tab_ref, q_ref, k_ref, v_ref, o_ref, q4_sc, t_even, t_odd, acc_sc,
                   *, tq, tk, seq, lambda_init):
    qi = pl.program_id(1)
    r = tk // tq
    nkv = seq // tk
    kd = qi // r
    var_d = qi % r
    i0 = qi * tq
    lane = _lane_iota((1, LANES))
    first = lane < HEAD_DIM
    npair = A_HEADS // 2
    ones11 = jnp.ones((1, 1), jnp.int32)

    for p in range(npair):
        qp = q_ref[0, :, p * LANES:(p + 1) * LANES]
        for g in range(4):
            lo = g * A_QK_DIM
            q4_sc[p, g] = jnp.where((lane >= lo) & (lane < lo + A_QK_DIM), qp, jnp.zeros_like(qp))

    variants = [jnp.where(c < kd, r, jnp.where(c > kd, r + 1, var_d)) for c in range(nkv)]
    gaps = [(jnp.where(c == kd, 0, jnp.abs(i0 - c * tk)) * ones11).astype(F32) for c in range(nkv)]
    nlt = tk // LANES

    for p in range(npair):
        cols = slice(p * LANES, (p + 1) * LANES)

        def offsets(g, p=p):
            return [gap * (-SLOPES_A[2 * p + g // 2] * LOG2E) for gap in gaps]

        def logits_chunk(g, c, offs, mx, p=p, cols=cols):
            h = 2 * p + g // 2
            t_sc = t_odd if g % 2 else t_even
            s = _nt_dot(q4_sc[p, g], k_ref[0, c * tk:(c + 1) * tk, cols])
            t = s + tab_ref[h, variants[c]]
            t_sc[:, c * tk:(c + 1) * tk] = t
            cm = t[:, 0:LANES]
            for lt in range(1, nlt):
                cm = jnp.maximum(cm, t[:, lt * LANES:(lt + 1) * LANES])
            cm = cm + offs[c]
            return cm if mx is None else jnp.maximum(mx, cm)

        def row_max(mx):
            return jnp.broadcast_to(jnp.max(mx, axis=1, keepdims=True), (tq, LANES))

        def pv_chunk(g, c, offs, mb, acc, cols=cols):
            t_sc = t_odd if g % 2 else t_even
            keep = first if g // 2 == 0 else jnp.logical_not(first)
            mbc = mb - offs[c]
            pr = [jnp.exp2(t_sc[:, c * tk + lt * LANES:c * tk + (lt + 1) * LANES] - mbc).astype(BF16)
                  for lt in range(nlt)]
            vb = v_ref[0, c * tk:(c + 1) * tk, cols]
            v_aug = jnp.where(keep, vb, jnp.ones_like(vb))
            pv = jnp.dot(jnp.concatenate(pr, axis=1), v_aug, preferred_element_type=F32)
            return pv if acc is None else acc + pv

        offs = [offsets(g) for g in range(4)]
        mx = None
        for c in range(nkv):
            mx = logits_chunk(0, c, offs[0], mx)
        mb = row_max(mx)
        for g in range(4):
            mx, acc = None, None
            for c in range(nkv):
                if g + 1 < 4:
                    mx = logits_chunk(g + 1, c, offs[g + 1], mx)
                acc = pv_chunk(g, c, offs[g], mb, acc)
            acc_sc[4 * p + g] = acc
            if g + 1 < 4:
                mb = row_max(mx)

    lf = lam_ref[...]
    lam_full = (jnp.exp(jnp.sum(lf[0:1] * lf[1:2], axis=(0, 1), keepdims=True))
                - jnp.exp(jnp.sum(lf[2:3] * lf[3:4], axis=(0, 1), keepdims=True)) + lambda_init)
    for p in range(npair):
        d_heads = []
        for a in range(2):
            o_m = []
            for m2 in range(2):
                acc = acc_sc[4 * p + 2 * a + m2]
                o_m.append(acc / pltpu.roll(acc, HEAD_DIM, 1))
            d_heads.append(o_m[0] - lam_full * o_m[1])
        dp = jnp.where(first, d_heads[0], d_heads[1])
        sq = dp * dp
        ms0 = jnp.sum(jnp.where(first, sq, 0.0), axis=1, keepdims=True) * (1.0 / HEAD_DIM)
        ms1 = jnp.sum(jnp.where(first, 0.0, sq), axis=1, keepdims=True) * (1.0 / HEAD_DIM)
        ms = jnp.where(first, ms0, ms1)
        out = dp * lax.rsqrt(ms + LN_EPS) * g_ref[...] * (1.0 - lambda_init)
        o_ref[0, :, p * LANES:(p + 1) * LANES] = out.astype(o_ref.dtype)


def _attention_a(proj, lam_l, g2, lambda_init, tq=256, tk=512):
    bsz, t, _ = proj.shape
    tab = _attn_a_tables(tq, tk)
    nvar = tab.shape[1]
    npair = A_HEADS // 2
    kern = functools.partial(_attn_a_kernel, tq=tq, tk=tk, seq=t, lambda_init=lambda_init)
    return pl.pallas_call(
        kern,
        grid=(bsz, t // tq),
        in_specs=[pl.BlockSpec((4, A_QK_DIM), lambda b, i: (0, 0)),
                  pl.BlockSpec((1, LANES), lambda b, i: (0, 0)),
                  pl.BlockSpec((A_HEADS, nvar, tq, tk), lambda b, i: (0, 0, 0, 0),
                               pipeline_mode=pl.Buffered(1)),
                  pl.BlockSpec((1, tq, A_W), lambda b, i: (b, i, 0)),
                  pl.BlockSpec((1, t, A_W), lambda b, i: (b, 0, 1)),
                  pl.BlockSpec((1, t, A_W), lambda b, i: (b, 0, 2))],
        out_specs=pl.BlockSpec((1, tq, A_W), lambda b, i: (b, i, 0)),
        out_shape=jax.ShapeDtypeStruct((bsz, t, A_W), BF16),
        scratch_shapes=[pltpu.VMEM((npair, 4, tq, LANES), BF16),
                        pltpu.VMEM((tq, t), F32),
                        pltpu.VMEM((tq, t), F32),
                        pltpu.VMEM((2 * A_HEADS, tq, LANES), F32)],
        compiler_params=pltpu.CompilerParams(dimension_semantics=("parallel", "arbitrary"),
                                             vmem_limit_bytes=VMEM_LIMIT),
        name="attn_diff",
    )(lam_l, g2, tab, proj, proj, proj)


def _attn_b_kernel(sl_ref, q_ref, k_ref, v_ref, o_ref, qf, kf, vf, q4, k4, v4, m_st, l_st, o_st, bias_sc,
                   *, seq, tq, win, unroll):
    lane = _lane_iota((1, LANES))
    first = lane < HEAD_DIM
    halves = (first, jnp.logical_not(first))
    chunk = 512

    def prep(c, carry):
        r0 = pl.multiple_of(c * chunk, chunk)
        rows = pl.ds(r0, chunk)
        qf[rows, :] = q_ref[0, rows, :].astype(F32)
        kf[rows, :] = k_ref[0, rows, :].astype(F32)
        vf[rows, :] = v_ref[0, rows, :].astype(F32)
        m_st[rows, :] = jnp.full((chunk, LANES), NEG, F32)
        l_st[rows, :] = jnp.zeros((chunk, LANES), F32)
        o_st[rows, :] = jnp.zeros((chunk, LANES), F32)
        return carry

    lax.fori_loop(0, seq // chunk, prep, 0)

    len4 = seq // 4
    for src_ref, dst_ref in ((qf, q4), (kf, k4), (vf, v4)):
        for r4 in range(4):
            for c0 in range(0, len4, chunk):
                dst_ref[r4 * len4 + c0:r4 * len4 + c0 + chunk, :] = src_ref[pl.ds(r4 + 4 * c0, chunk, stride=4), :]

    relb = (lax.broadcasted_iota(jnp.int32, (tq, win), 0) - lax.broadcasted_iota(jnp.int32, (tq, win), 1))

    for d in B_DILATIONS:
        length = seq // d
        nblk = length // tq

        for case in range(3):
            adist = jnp.abs(relb + case * B_HALF)
            adf = adist.astype(F32)
            for a in range(2):
                sl = sl_ref[0, a:a + 1, 0:1] * float(d)
                bias_sc[case, a] = jnp.where(adist <= B_HALF, -sl * adf, NEG)

        def state_rows(res, j, size, d=d):
            return pl.ds(j, size) if d == 1 else pl.ds(res + j * d, size, stride=d)

        def data_rows(res, j, size, d=d):
            if d == 1:
                return pl.ds(j, size)
            if d == 4:
                return pl.ds(res * len4 + j, size)
            return pl.ds((res % 4) * len4 + res // 4 + j * (d // 4), size, stride=d // 4)

        q_src, k_src, v_src = (qf, kf, vf) if d == 1 else (q4, k4, v4)

        def body(it, carry, length=length, nblk=nblk, state_rows=state_rows, data_rows=data_rows,
                 q_src=q_src, k_src=k_src, v_src=v_src):
            rqs, cases, qs, ks, vs = [], [], [], [], []
            for u in range(unroll):
                idx = it * unroll + u
                res = idx // nblk
                j0 = (idx % nblk) * tq
                start = jnp.clip(j0 - B_HALF, 0, length - win)
                rqs.append(state_rows(res, j0, tq))
                cases.append((j0 - start) // B_HALF)
                qs.append(q_src[data_rows(res, j0, tq), :])
                ks.append(k_src[data_rows(res, start, win), :].astype(BF16))
                vs.append(v_src[data_rows(res, start, win), :])
            items = [(u, a) for u in range(unroll) for a in range(2)]
            ss = [_nt_dot(jnp.where(halves[a], qs[u], 0.0).astype(BF16), ks[u]) for u, a in items]
            ts = [ss[i] + bias_sc[cases[u], a] for i, (u, a) in enumerate(items)]
            mbs = [jnp.max(t, axis=1, keepdims=True) for t in ts]
            prs = [jnp.exp2(t - mb).astype(BF16) for t, mb in zip(ts, mbs)]
            pvs = [jnp.dot(prs[i], jnp.where(halves[a], vs[u], 1.0).astype(BF16), preferred_element_type=F32)
                   for i, (u, a) in enumerate(items)]
            m_prev = [m_st[rq, :] for rq in rqs]
            l_prev = [l_st[rq, :] for rq in rqs]
            o_prev = [o_st[rq, :] for rq in rqs]
            for u in range(unroll):
                pv0, pv1 = pvs[2 * u], pvs[2 * u + 1]
                o_blk = jnp.where(first, pv0, pv1)
                l_blk = pltpu.roll(jnp.where(first, pv1, pv0), HEAD_DIM, 1)
                m_blk = jnp.where(first, mbs[2 * u], mbs[2 * u + 1])
                m_new = jnp.maximum(m_prev[u], m_blk)
                a_prev = jnp.exp2(m_prev[u] - m_new)
                a_blk = jnp.exp2(m_blk - m_new)
                l_st[rqs[u], :] = a_prev * l_prev[u] + a_blk * l_blk
                o_st[rqs[u], :] = a_prev * o_prev[u] + a_blk * o_blk
                m_st[rqs[u], :] = m_new
            return carry

        lax.fori_loop(0, seq // tq // unroll, body, 0)

    def fin(c, carry):
        r0 = pl.multiple_of(c * chunk, chunk)
        rows = pl.ds(r0, chunk)
        o_ref[0, rows, :] = (o_st[rows, :] / l_st[rows, :]).astype(o_ref.dtype)
        return carry

    lax.fori_loop(0, seq // chunk, fin, 0)


def _attention_b(proj, tq=128, win=256, unroll=4):
    bsz, t, _ = proj.shape
    npair = B_HEADS // 2
    sl = np.zeros((npair, 8, LANES), np.float32)
    for h, s in enumerate(SLOPES_B):
        sl[h // 2, h % 2, :] = s * LOG2E
    qb0 = 3 * A_W // LANES
    kern = functools.partial(_attn_b_kernel, seq=t, tq=tq, win=win, unroll=unroll)
    seq_spec = lambda off: pl.BlockSpec((1, t, LANES), lambda b, p, off=off: (b, 0, off + p))
    return pl.pallas_call(
        kern,
        grid=(bsz, npair),
        in_specs=[pl.BlockSpec((1, 8, LANES), lambda b, p: (p, 0, 0)),
                  seq_spec(qb0), seq_spec(qb0 + B_W // LANES), seq_spec(qb0 + 2 * B_W // LANES)],
        out_specs=pl.BlockSpec((1, t, LANES), lambda b, p: (b, 0, p)),
        out_shape=jax.ShapeDtypeStruct((bsz, t, B_W), BF16),
        scratch_shapes=[pltpu.VMEM((t, LANES), F32) for _ in range(9)] + [pltpu.VMEM((3, 2, tq, win), F32)],
        compiler_params=pltpu.CompilerParams(dimension_semantics=("parallel", "arbitrary"),
                                             vmem_limit_bytes=VMEM_LIMIT),
        name="attn_dilated",
    )(jnp.asarray(sl), proj, proj, proj)


def _attn_c_kernel(sink_ref, q_ref, k_ref, v_ref, o_ref, *, seq, tq, win):
    qi = pl.program_id(1)
    i0 = qi * tq
    start = pl.multiple_of(jnp.clip(i0 - C_HALF_WINDOW, 0, seq - win), C_HALF_WINDOW)
    k2 = k_ref[0, pl.ds(start, win), :].astype(F32)
    v2 = v_ref[0, pl.ds(start, win), :]
    k_same = k2.astype(BF16)
    k_swap = pltpu.roll(k2, HEAD_DIM, 1).astype(BF16)
    rel = (lax.broadcasted_iota(jnp.int32, (tq, win), 0) - lax.broadcasted_iota(jnp.int32, (tq, win), 1)
           + (i0 - start))
    dist = jnp.abs(rel)
    valid = dist <= C_HALF_WINDOW
    distf = dist.astype(F32)
    lane = _lane_iota((1, LANES))
    first = lane < HEAD_DIM
    halves = (first, jnp.logical_not(first))
    rep = C_HEADS // C_KV_HEADS
    heads = range(C_HEADS)
    kv_of = [h // rep for h in heads]
    qps = [q_ref[0, :, p * LANES:(p + 1) * LANES] for p in range(C_HEADS // 2)]
    qms = [jnp.where(halves[h % 2], qps[h // 2], jnp.zeros_like(qps[h // 2])) for h in heads]
    ss = [_nt_dot(qms[h], k_same if h % 2 == kv_of[h] else k_swap) for h in heads]
    ts = [jnp.where(valid, ss[h] - (SLOPES_C[h] * LOG2E) * distf, NEG) for h in heads]
    ms = [jnp.max(t, axis=1, keepdims=True) for t in ts]
    prs = [jnp.exp2(ts[h] - ms[h]) for h in heads]
    ls = [jnp.sum(pr, axis=1, keepdims=True) for pr in prs]
    pvs = [jnp.dot(prs[h].astype(BF16), v2, preferred_element_type=F32) for h in heads]
    pvs = [pvs[h] if h % 2 == kv_of[h] else pltpu.roll(pvs[h], HEAD_DIM, 1) for h in heads]
    for p in range(C_HEADS // 2):
        h0, h1 = 2 * p, 2 * p + 1
        m_pair = jnp.where(first, ms[h0], ms[h1])
        l_pair = jnp.where(first, ls[h0], ls[h1])
        o_pair = jnp.where(first, pvs[h0], pvs[h1])
        sk = sink_ref[:, p * LANES:(p + 1) * LANES] * LOG2E
        mx = jnp.maximum(m_pair, sk)
        a_ = jnp.exp2(m_pair - mx)
        out = o_pair * a_ / (l_pair * a_ + jnp.exp2(sk - mx))
        o_ref[0, :, p * LANES:(p + 1) * LANES] = out.astype(o_ref.dtype)


def _attention_c(proj, sink_lanes, tq=128, win=384):
    bsz, t, _ = proj.shape
    kern = functools.partial(_attn_c_kernel, seq=t, tq=tq, win=win)
    qc0 = (3 * A_W + 3 * B_W) // C_QW
    kc0 = (3 * A_W + 3 * B_W + C_QW) // C_KW
    return pl.pallas_call(
        kern,
        grid=(bsz, t // tq),
        in_specs=[pl.BlockSpec((1, C_QW), lambda b, i: (0, 0)),
                  pl.BlockSpec((1, tq, C_QW), lambda b, i: (b, i, qc0)),
                  pl.BlockSpec((1, t, C_KW), lambda b, i: (b, 0, kc0)),
                  pl.BlockSpec((1, t, C_KW), lambda b, i: (b, 0, kc0 + 1))],
        out_specs=pl.BlockSpec((1, tq, C_QW), lambda b, i: (b, i, 0)),
        out_shape=jax.ShapeDtypeStruct((bsz, t, C_QW), BF16),
        compiler_params=pltpu.CompilerParams(dimension_semantics=("parallel", "arbitrary"),
                                             vmem_limit_bytes=VMEM_LIMIT),
        name="attn_window",
    )(sink_lanes, proj, proj, proj)


def _layer_norm(y, g, b):
    mu = jnp.mean(y, axis=-1, keepdims=True)
    yc = y - mu
    var = jnp.mean(yc * yc, axis=-1, keepdims=True)
    return yc * lax.rsqrt(var + LN_EPS) * g + b


def _tail_kernel(oa_ref, ob_ref, oc_ref, x_ref, mod_ref, wo_ref, wgu_ref, wd_ref, lng_ref, lnb_ref, o_ref,
                 *, alpha, hidden, ck):
    mix = (jnp.dot(oa_ref[0], wo_ref[0:A_W, :], preferred_element_type=F32)
           + jnp.dot(ob_ref[0], wo_ref[A_W:A_W + B_W, :], preferred_element_type=F32)
           + jnp.dot(oc_ref[0], wo_ref[A_W + B_W:, :], preferred_element_type=F32))
    g1 = mod_ref[0, 2:3, :]
    x1 = _layer_norm(alpha * x_ref[0] + (1.0 + g1) * mix, lng_ref[0:1, :], lnb_ref[0:1, :])
    sh2 = mod_ref[0, 3:4, :]
    sc2 = mod_ref[0, 4:5, :]
    g2 = mod_ref[0, 5:6, :]
    h2 = (x1 * (1.0 + sc2) + sh2).astype(BF16)
    acc = jnp.zeros(x1.shape, F32)
    for c0 in range(0, hidden, ck):
        gate = jnp.dot(h2, wgu_ref[:, c0:c0 + ck], preferred_element_type=F32)
        up = jnp.dot(h2, wgu_ref[:, hidden + c0:hidden + c0 + ck], preferred_element_type=F32)
        act = (gate * jax.nn.sigmoid(gate) * up).astype(BF16)
        acc = acc + jnp.dot(act, wd_ref[c0:c0 + ck, :], preferred_element_type=F32)
    o_ref[0] = _layer_norm(alpha * x1 + (1.0 + g2) * acc, lng_ref[1:2, :], lnb_ref[1:2, :])


def _tail(oa, ob, oc, x, mod6, wo_bf, wgu_bf, wd_bf, ln_g, ln_b, alpha, tm=512):
    bsz, t, d = x.shape
    hidden = wd_bf.shape[0]
    kern = functools.partial(_tail_kernel, alpha=alpha, hidden=hidden, ck=256)
    row = lambda w: pl.BlockSpec((1, tm, w), lambda b, i: (b, i, 0))
    whole = lambda shp: pl.BlockSpec(shp, lambda b, i: (0, 0), pipeline_mode=pl.Buffered(1))
    return pl.pallas_call(
        kern,
        grid=(bsz, t // tm),
        in_specs=[row(A_W), row(B_W), row(C_QW), row(d),
                  pl.BlockSpec((1, 6, d), lambda b, i: (b, 0, 0)),
                  whole(wo_bf.shape), whole(wgu_bf.shape), whole(wd_bf.shape),
                  pl.BlockSpec((2, d), lambda b, i: (0, 0)),
                  pl.BlockSpec((2, d), lambda b, i: (0, 0))],
        out_specs=row(d),
        out_shape=jax.ShapeDtypeStruct((bsz, t, d), F32),
        compiler_params=pltpu.CompilerParams(dimension_semantics=("parallel", "arbitrary"),
                                             vmem_limit_bytes=VMEM_LIMIT),
        name="outproj_ffn",
    )(oa, ob, oc, x, mod6, wo_bf, wgu_bf, wd_bf, ln_g, ln_b)


def _q_col_scale():
    cs = np.ones((1, IN_WIDTH), np.float32)
    cs[0, 0:A_W] = A_QK_DIM ** -0.5 * LOG2E
    cs[0, 3 * A_W:3 * A_W + B_W] = HEAD_DIM ** -0.5 * LOG2E
    cs[0, 3 * A_W + 3 * B_W:3 * A_W + 3 * B_W + C_QW] = HEAD_DIM ** -0.5 * LOG2E
    return jnp.asarray(cs)


def kernel(x, c, w_ada, b_ada, w_in, lam, subln_g, sink, w_out, ln_g, ln_b, w_gu, w_down):
    depth = w_ada.shape[0]
    bsz, _, d = x.shape
    alpha = (2 * depth) ** 0.25
    mod = _modulation(c, w_ada, b_ada).reshape(depth, bsz, 6, d)
    col_scale = _q_col_scale()
    for layer in range(depth):
        lambda_init = 0.8 - 0.6 * math.exp(-0.3 * layer)
        proj = _in_projection(x, mod[layer], w_in[layer].astype(BF16), col_scale)
        g2 = jnp.tile(subln_g[layer], 2).reshape(1, LANES)
        oa = _attention_a(proj, lam[layer], g2, lambda_init)
        ob = _attention_b(proj)
        oc = _attention_c(proj, jnp.repeat(sink[layer], HEAD_DIM).reshape(1, C_QW))
        x = _tail(oa, ob, oc, x, mod[layer], w_out[layer].astype(BF16), w_gu[layer].astype(BF16),
                  w_down[layer].astype(BF16), ln_g[layer], ln_b[layer], alpha)
    return x
```

```python
import functools
import math

import numpy as np
import jax
import jax.numpy as jnp
from jax import lax
from jax.experimental import pallas as pl
from jax.experimental.pallas import tpu as pltpu

F32 = jnp.float32
BF16 = jnp.bfloat16

HEAD_DIM = 64
A_HEADS = 4
A_QK_DIM = 32
B_HEADS = 6
B_DILATIONS = (1, 4, 16)
B_HALF = 64
C_HEADS = 6
C_KV_HEADS = 2
C_HALF_WINDOW = 128
N_ATTN_HEADS = 16
LN_EPS = 1e-5
NEG = -1e30
LOG2E = math.log2(math.e)

LANES = 128
A_W = A_HEADS * HEAD_DIM
B_W = B_HEADS * HEAD_DIM
C_QW = C_HEADS * HEAD_DIM
C_KW = C_KV_HEADS * HEAD_DIM
IN_WIDTH = 3 * A_W + 3 * B_W + C_QW + 2 * C_KW
VMEM_LIMIT = 56 * 1024 * 1024

_S = 2.0 ** (-8.0 * np.arange(1, N_ATTN_HEADS + 1) / N_ATTN_HEADS)
SLOPES_C = [float(v) for v in _S[:C_HEADS]]
SLOPES_A = [float(v) for v in _S[C_HEADS:C_HEADS + A_HEADS]]
SLOPES_B = [float(v) for v in _S[C_HEADS + A_HEADS:]]


def _nt_dot(a, b):
    return lax.dot_general(a, b, (((1,), (1,)), ((), ())), preferred_element_type=F32)


def _lane_iota(shape):
    return lax.broadcasted_iota(jnp.int32, shape, len(shape) - 1)


def _mod_kernel(c_ref, w_ref, b_ref, o_ref):
    c = c_ref[...]
    a = (c * jax.nn.sigmoid(c)).astype(BF16)
    o_ref[0] = jnp.dot(a, w_ref[0].astype(BF16), preferred_element_type=F32) + b_ref[0]


def _modulation(c, w_ada, b_ada):
    depth, d, n = w_ada.shape
    bsz = c.shape[0]
    tn = 1536
    return pl.pallas_call(
        _mod_kernel,
        grid=(depth, n // tn),
        in_specs=[pl.BlockSpec((bsz, d), lambda l, j: (0, 0)),
                  pl.BlockSpec((1, d, tn), lambda l, j: (l, 0, j)),
                  pl.BlockSpec((1, 1, tn), lambda l, j: (l, 0, j))],
        out_specs=pl.BlockSpec((1, bsz, tn), lambda l, j: (l, 0, j)),
        out_shape=jax.ShapeDtypeStruct((depth, bsz, n), F32),
        compiler_params=pltpu.CompilerParams(dimension_semantics=("arbitrary", "arbitrary"),
                                             vmem_limit_bytes=VMEM_LIMIT),
        name="adaln_mod",
    )(c, w_ada, b_ada.reshape(depth, 1, n))


def _proj_kernel(x_ref, mod_ref, w_ref, cs_ref, o_ref, *, tn):
    sh = mod_ref[0, 0:1, :]
    sc = mod_ref[0, 1:2, :]
    h = (x_ref[0] * (1.0 + sc) + sh).astype(BF16)
    for n0 in range(0, o_ref.shape[2], tn):
        acc = jnp.dot(h, w_ref[:, n0:n0 + tn], preferred_element_type=F32)
        o_ref[0, :, n0:n0 + tn] = (acc * cs_ref[:, n0:n0 + tn]).astype(BF16)


def _in_projection(x, mod6, w_in_bf, col_scale, tm=512):
    bsz, t, d = x.shape
    n = w_in_bf.shape[1]
    return pl.pallas_call(
        functools.partial(_proj_kernel, tn=512),
        grid=(bsz, t // tm),
        in_specs=[pl.BlockSpec((1, tm, d), lambda b, i: (b, i, 0)),
                  pl.BlockSpec((1, 6, d), lambda b, i: (b, 0, 0)),
                  pl.BlockSpec((d, n), lambda b, i: (0, 0), pipeline_mode=pl.Buffered(1)),
                  pl.BlockSpec((1, n), lambda b, i: (0, 0))],
        out_specs=pl.BlockSpec((1, tm, n), lambda b, i: (b, i, 0)),
        out_shape=jax.ShapeDtypeStruct((bsz, t, n), BF16),
        compiler_params=pltpu.CompilerParams(dimension_semantics=("parallel", "arbitrary"),
                                             vmem_limit_bytes=VMEM_LIMIT),
        name="in_proj",
    )(x, mod6, w_in_bf, col_scale)


def _attn_a_tables(tq, tk):
    rel = (np.arange(tq)[:, None] - np.arange(tk)[None, :]).astype(np.float64)
    tabs = [np.stack([-s * LOG2E * np.abs(rel + c * tq) for c in range(tk // tq)]) for s in SLOPES_A]
    return jnp.asarray(np.stack(tabs).astype(np.float32))


def _alibi_lane(head):
    return 3 * A_QK_DIM if head % 2 == 0 else 0


def _attn_a_alibi_lanes(seq):
    pos = np.arange(seq, dtype=np.float64)
    bf = jnp.bfloat16
    kside = np.zeros((A_HEADS, seq, LANES), np.float32)
    qside = np.zeros((A_HEADS, seq, LANES), np.float32)
    for h, s in enumerate(SLOPES_A):
        a0 = _alibi_lane(h)
        x = (pos * (s * LOG2E)).astype(np.float32)
        parts, rest = [], x
        for _ in range(3):
            part = np.asarray(rest.astype(bf), np.float32)
            parts.append(part)
            rest = (rest - part).astype(np.float32)
        for i, part in enumerate(parts):
            kside[h, :, a0 + i] = part
            qside[h, :, a0 + 3 + i] = -part
        kside[h, :, a0 + 3:a0 + 6] = 1.0
        qside[h, :, a0:a0 + 3] = 1.0
    return jnp.asarray(kside, BF16), jnp.asarray(qside, BF16)


def _attn_a_kernel(lam_ref, g_ref, tab_ref, kside_ref, qside_ref, q_ref, k_ref, v_ref, o_ref,
                   kaug, vaug, qv_sc, t_even, t_odd, acc_sc, *, tq, tk, seq, lambda_init):
    qi = pl.program_id(1)
    r = tk // tq
    nkv = seq // tk
    kd = qi // r
    var_d = qi % r
    i0 = pl.multiple_of(qi * tq, tq)
    lane = _lane_iota((1, LANES))
    first = lane < HEAD_DIM
    nmap = 2 * A_HEADS
    nlt = tk // LANES
    rows_per_build = 512

    def group_lanes(gm):
        lo = A_QK_DIM * (gm % 4)
        return (lane >= lo) & (lane < lo + A_QK_DIM)

    @pl.when(qi == 0)
    def _build_kv_copies():
        for rc in range(seq // rows_per_build):
            rows = slice(rc * rows_per_build, (rc + 1) * rows_per_build)
            for p in range(A_HEADS // 2):
                kp = k_ref[0, rows, p * LANES:(p + 1) * LANES]
                vp = v_ref[0, rows, p * LANES:(p + 1) * LANES]
                for a in range(2):
                    vaug[2 * p + a, rows, :] = jnp.where(first if a == 0 else jnp.logical_not(first), vp,
                                                         jnp.ones_like(vp))
                for g in range(4):
                    gm = 4 * p + g
                    kaug[gm, rows, :] = jnp.where(group_lanes(gm), kp, kside_ref[gm // 2, rows, :])

    for p in range(A_HEADS // 2):
        qp = q_ref[0, :, p * LANES:(p + 1) * LANES]
        for g in range(4):
            gm = 4 * p + g
            side = qside_ref[gm // 2, pl.ds(i0, tq), :]
            qv_sc[gm, 0] = jnp.where(group_lanes(gm), qp, jnp.zeros_like(qp))
            qv_sc[gm, 1] = jnp.where(group_lanes(gm), qp, side)
            qv_sc[gm, 2] = jnp.where(group_lanes(gm), qp, -side)

    chunk_ids = [kd] + [(kd + cc) & (nkv - 1) for cc in range(1, nkv)]
    starts = [pl.multiple_of(cid * tk, tk) for cid in chunk_ids]
    q_variant = [0] + [jnp.where(cid < kd, 1, 2) for cid in chunk_ids[1:]]

    def logits_chunk(gm, cc, mx):
        t_sc = t_odd if gm % 2 else t_even
        s = _nt_dot(qv_sc[gm, q_variant[cc]], kaug[gm, pl.ds(starts[cc], tk), :])
        if cc == 0:
            s = s + tab_ref[gm // 2, var_d]
        t_sc[:, cc * tk:(cc + 1) * tk] = s
        cm = s[:, 0:LANES]
        for lt in range(1, nlt):
            cm = jnp.maximum(cm, s[:, lt * LANES:(lt + 1) * LANES])
        return cm if mx is None else jnp.maximum(mx, cm)

    def row_max(mx):
        return jnp.broadcast_to(jnp.max(mx, axis=1, keepdims=True), (tq, LANES))

    def pv_chunk(gm, cc, mb, acc):
        t_sc = t_odd if gm % 2 else t_even
        blocks = []
        for rb in range(0, tq, 64):
            blocks.append(jnp.concatenate(
                [jnp.exp2(t_sc[rb:rb + 64, cc * tk + lt * LANES:cc * tk + (lt + 1) * LANES]
                          - mb[rb:rb + 64, :]).astype(BF16) for lt in range(nlt)], axis=1))
        pv = jnp.dot(jnp.concatenate(blocks, axis=0), vaug[gm // 2, pl.ds(starts[cc], tk), :],
                     preferred_element_type=F32)
        return pv if acc is None else acc + pv

    mx = None
    for cc in range(nkv):
        mx = logits_chunk(0, cc, mx)
    mb = row_max(mx)
    for gm in range(nmap):
        mx, acc = None, None
        for cc in range(nkv):
            if gm + 1 < nmap:
                mx = logits_chunk(gm + 1, cc, mx)
            acc = pv_chunk(gm, cc, mb, acc)
        acc_sc[gm] = acc
        if gm + 1 < nmap:
            mb = row_max(mx)

    lf = lam_ref[...]
    lam_full = (jnp.exp(jnp.sum(lf[0:1] * lf[1:2], axis=(0, 1), keepdims=True))
                - jnp.exp(jnp.sum(lf[2:3] * lf[3:4], axis=(0, 1), keepdims=True)) + lambda_init)
    for p in range(A_HEADS // 2):
        d_heads = []
        for a in range(2):
            o_m = []
            for m2 in range(2):
                acc = acc_sc[4 * p + 2 * a + m2]
                o_m.append(acc / pltpu.roll(acc, HEAD_DIM, 1))
            d_heads.append(o_m[0] - lam_full * o_m[1])
        dp = jnp.where(first, d_heads[0], d_heads[1])
        sq = dp * dp
        ms0 = jnp.sum(jnp.where(first, sq, 0.0), axis=1, keepdims=True) * (1.0 / HEAD_DIM)
        ms1 = jnp.sum(jnp.where(first, 0.0, sq), axis=1, keepdims=True) * (1.0 / HEAD_DIM)
        ms = jnp.where(first, ms0, ms1)
        out = dp * lax.rsqrt(ms + LN_EPS) * g_ref[...] * (1.0 - lambda_init)
        o_ref[0, :, p * LANES:(p + 1) * LANES] = out.astype(o_ref.dtype)


def _attention_a(proj, lam_l, g2, lambda_init, tq=256, tk=512):
    bsz, t, _ = proj.shape
    tab = _attn_a_tables(tq, tk)
    kside, qside = _attn_a_alibi_lanes(t)
    nmap = 2 * A_HEADS
    kern = functools.partial(_attn_a_kernel, tq=tq, tk=tk, seq=t, lambda_init=lambda_init)
    return pl.pallas_call(
        kern,
        grid=(bsz, t // tq),
        in_specs=[pl.BlockSpec((4, A_QK_DIM), lambda b, i: (0, 0)),
                  pl.BlockSpec((1, LANES), lambda b, i: (0, 0)),
                  pl.BlockSpec(tab.shape, lambda b, i: (0, 0, 0, 0), pipeline_mode=pl.Buffered(1)),
                  pl.BlockSpec(kside.shape, lambda b, i: (0, 0, 0), pipeline_mode=pl.Buffered(1)),
                  pl.BlockSpec(qside.shape, lambda b, i: (0, 0, 0), pipeline_mode=pl.Buffered(1)),
                  pl.BlockSpec((1, tq, A_W), lambda b, i: (b, i, 0)),
                  pl.BlockSpec((1, t, A_W), lambda b, i: (b, 0, 1)),
                  pl.BlockSpec((1, t, A_W), lambda b, i: (b, 0, 2))],
        out_specs=pl.BlockSpec((1, tq, A_W), lambda b, i: (b, i, 0)),
        out_shape=jax.ShapeDtypeStruct((bsz, t, A_W), BF16),
        scratch_shapes=[pltpu.VMEM((nmap, t, LANES), BF16),
                        pltpu.VMEM((A_HEADS, t, LANES), BF16),
                        pltpu.VMEM((nmap, 3, tq, LANES), BF16),
                        pltpu.VMEM((tq, t), F32),
                        pltpu.VMEM((tq, t), F32),
                        pltpu.VMEM((nmap, tq, LANES), F32)],
        compiler_params=pltpu.CompilerParams(dimension_semantics=("parallel", "arbitrary"),
                                             vmem_limit_bytes=VMEM_LIMIT),
        name="attn_diff",
    )(lam_l, g2, tab, kside, qside, proj, proj, proj)


def _attn_b_kernel(sl_ref, q_ref, k_ref, v_ref, o_ref, qf, kf, vf, q4, k4, v4, m_st, l_st, o_st, bias_sc,
                   *, seq, tq, win, unroll):
    lane = _lane_iota((1, LANES))
    first = lane < HEAD_DIM
    halves = (first, jnp.logical_not(first))
    chunk = 512

    def prep(c, carry):
        r0 = pl.multiple_of(c * chunk, chunk)
        rows = pl.ds(r0, chunk)
        qf[rows, :] = q_ref[0, rows, :].astype(F32)
        kf[rows, :] = k_ref[0, rows, :].astype(F32)
        vf[rows, :] = v_ref[0, rows, :].astype(F32)
        m_st[rows, :] = jnp.full((chunk, LANES), NEG, F32)
        l_st[rows, :] = jnp.zeros((chunk, LANES), F32)
        o_st[rows, :] = jnp.zeros((chunk, LANES), F32)
        return carry

    lax.fori_loop(0, seq // chunk, prep, 0)

    len4 = seq // 4
    for src_ref, dst_ref in ((qf, q4), (kf, k4), (vf, v4)):
        for r4 in range(4):
            for c0 in range(0, len4, chunk):
                dst_ref[r4 * len4 + c0:r4 * len4 + c0 + chunk, :] = src_ref[pl.ds(r4 + 4 * c0, chunk, stride=4), :]

    relb = (lax.broadcasted_iota(jnp.int32, (tq, win), 0) - lax.broadcasted_iota(jnp.int32, (tq, win), 1))

    for d in B_DILATIONS:
        length = seq // d
        nblk = length // tq

        for case in range(3):
            adist = jnp.abs(relb + case * B_HALF)
            adf = adist.astype(F32)
            for a in range(2):
                sl = sl_ref[0, a:a + 1, 0:1] * float(d)
                bias_sc[case, a] = jnp.where(adist <= B_HALF, -sl * adf, NEG)

        def state_rows(res, j, size, d=d):
            return pl.ds(j, size) if d == 1 else pl.ds(res + j * d, size, stride=d)

        def data_rows(res, j, size, d=d):
            if d == 1:
                return pl.ds(j, size)
            if d == 4:
                return pl.ds(res * len4 + j, size)
            return pl.ds((res % 4) * len4 + res // 4 + j * (d // 4), size, stride=d // 4)

        q_src, k_src, v_src = (qf, kf, vf) if d == 1 else (q4, k4, v4)

        def body(it, carry, length=length, nblk=nblk, state_rows=state_rows, data_rows=data_rows,
                 q_src=q_src, k_src=k_src, v_src=v_src):
            rqs, cases, qs, ks, vs = [], [], [], [], []
            for u in range(unroll):
                idx = it * unroll + u
                res = idx // nblk
                j0 = (idx % nblk) * tq
                start = jnp.clip(j0 - B_HALF, 0, length - win)
                rqs.append(state_rows(res, j0, tq))
                cases.append((j0 - start) // B_HALF)
                qs.append(q_src[data_rows(res, j0, tq), :])
                ks.append(k_src[data_rows(res, start, win), :].astype(BF16))
                vs.append(v_src[data_rows(res, start, win), :])
            items = [(u, a) for u in range(unroll) for a in range(2)]
            ss = [_nt_dot(jnp.where(halves[a], qs[u], 0.0).astype(BF16), ks[u]) for u, a in items]
            ts = [ss[i] + bias_sc[cases[u], a] for i, (u, a) in enumerate(items)]
            mbs = [jnp.max(t, axis=1, keepdims=True) for t in ts]
            prs = [jnp.exp2(t - mb).astype(BF16) for t, mb in zip(ts, mbs)]
            pvs = [jnp.dot(prs[i], jnp.where(halves[a], vs[u], 1.0).astype(BF16), preferred_element_type=F32)
                   for i, (u, a) in enumerate(items)]
            m_prev = [m_st[rq, :] for rq in rqs]
            l_prev = [l_st[rq, :] for rq in rqs]
            o_prev = [o_st[rq, :] for rq in rqs]
            for u in range(unroll):
                pv0, pv1 = pvs[2 * u], pvs[2 * u + 1]
                o_blk = jnp.where(first, pv0, pv1)
                l_blk = pltpu.roll(jnp.where(first, pv1, pv0), HEAD_DIM, 1)
                m_blk = jnp.where(first, mbs[2 * u], mbs[2 * u + 1])
                m_new = jnp.maximum(m_prev[u], m_blk)
                a_prev = jnp.exp2(m_prev[u] - m_new)
                a_blk = jnp.exp2(m_blk - m_new)
                l_st[rqs[u], :] = a_prev * l_prev[u] + a_blk * l_blk
                o_st[rqs[u], :] = a_prev * o_prev[u] + a_blk * o_blk
                m_st[rqs[u], :] = m_new
            return carry

        lax.fori_loop(0, seq // tq // unroll, body, 0)

    def fin(c, carry):
        r0 = pl.multiple_of(c * chunk, chunk)
        rows = pl.ds(r0, chunk)
        o_ref[0, rows, :] = (o_st[rows, :] / l_st[rows, :]).astype(o_ref.dtype)
        return carry

    lax.fori_loop(0, seq // chunk, fin, 0)


def _attention_b(proj, tq=128, win=256, unroll=4):
    bsz, t, _ = proj.shape
    npair = B_HEADS // 2
    sl = np.zeros((npair, 8, LANES), np.float32)
    for h, s in enumerate(SLOPES_B):
        sl[h // 2, h % 2, :] = s * LOG2E
    qb0 = 3 * A_W // LANES
    kern = functools.partial(_attn_b_kernel, seq=t, tq=tq, win=win, unroll=unroll)
    seq_spec = lambda off: pl.BlockSpec((1, t, LANES), lambda b, p, off=off: (b, 0, off + p))
    return pl.pallas_call(
        kern,
        grid=(bsz, npair),
        in_specs=[pl.BlockSpec((1, 8, LANES), lambda b, p: (p, 0, 0)),
                  seq_spec(qb0), seq_spec(qb0 + B_W // LANES), seq_spec(qb0 + 2 * B_W // LANES)],
        out_specs=pl.BlockSpec((1, t, LANES), lambda b, p: (b, 0, p)),
        out_shape=jax.ShapeDtypeStruct((bsz, t, B_W), BF16),
        scratch_shapes=[pltpu.VMEM((t, LANES), F32) for _ in range(9)] + [pltpu.VMEM((3, 2, tq, win), F32)],
        compiler_params=pltpu.CompilerParams(dimension_semantics=("parallel", "arbitrary"),
                                             vmem_limit_bytes=VMEM_LIMIT),
        name="attn_dilated",
    )(jnp.asarray(sl), proj, proj, proj)


def _attn_c_kernel(sink_ref, q_ref, k_ref, v_ref, o_ref, *, seq, tq, win):
    qi = pl.program_id(1)
    i0 = qi * tq
    start = pl.multiple_of(jnp.clip(i0 - C_HALF_WINDOW, 0, seq - win), C_HALF_WINDOW)
    k2 = k_ref[0, pl.ds(start, win), :].astype(F32)
    v2 = v_ref[0, pl.ds(start, win), :]
    k_same = k2.astype(BF16)
    k_swap = pltpu.roll(k2, HEAD_DIM, 1).astype(BF16)
    rel = (lax.broadcasted_iota(jnp.int32, (tq, win), 0) - lax.broadcasted_iota(jnp.int32, (tq, win), 1)
           + (i0 - start))
    dist = jnp.abs(rel)
    valid = dist <= C_HALF_WINDOW
    distf = dist.astype(F32)
    lane = _lane_iota((1, LANES))
    first = lane < HEAD_DIM
    halves = (first, jnp.logical_not(first))
    rep = C_HEADS // C_KV_HEADS
    heads = range(C_HEADS)
    kv_of = [h // rep for h in heads]
    qps = [q_ref[0, :, p * LANES:(p + 1) * LANES] for p in range(C_HEADS // 2)]
    qms = [jnp.where(halves[h % 2], qps[h // 2], jnp.zeros_like(qps[h // 2])) for h in heads]
    ss = [_nt_dot(qms[h], k_same if h % 2 == kv_of[h] else k_swap) for h in heads]
    ts = [jnp.where(valid, ss[h] - (SLOPES_C[h] * LOG2E) * distf, NEG) for h in heads]
    ms = [jnp.max(t, axis=1, keepdims=True) for t in ts]
    prs = [jnp.exp2(ts[h] - ms[h]) for h in heads]
    ls = [jnp.sum(pr, axis=1, keepdims=True) for pr in prs]
    pvs = [jnp.dot(prs[h].astype(BF16), v2, preferred_element_type=F32) for h in heads]
    pvs = [pvs[h] if h % 2 == kv_of[h] else pltpu.roll(pvs[h], HEAD_DIM, 1) for h in heads]
    for p in range(C_HEADS // 2):
        h0, h1 = 2 * p, 2 * p + 1
        m_pair = jnp.where(first, ms[h0], ms[h1])
        l_pair = jnp.where(first, ls[h0], ls[h1])
        o_pair = jnp.where(first, pvs[h0], pvs[h1])
        sk = sink_ref[:, p * LANES:(p + 1) * LANES] * LOG2E
        mx = jnp.maximum(m_pair, sk)
        a_ = jnp.exp2(m_pair - mx)
        out = o_pair * a_ / (l_pair * a_ + jnp.exp2(sk - mx))
        o_ref[0, :, p * LANES:(p + 1) * LANES] = out.astype(o_ref.dtype)


def _attention_c(proj, sink_lanes, tq=128, win=384):
    bsz, t, _ = proj.shape
    kern = functools.partial(_attn_c_kernel, seq=t, tq=tq, win=win)
    qc0 = (3 * A_W + 3 * B_W) // C_QW
    kc0 = (3 * A_W + 3 * B_W + C_QW) // C_KW
    return pl.pallas_call(
        kern,
        grid=(bsz, t // tq),
        in_specs=[pl.BlockSpec((1, C_QW), lambda b, i: (0, 0)),
                  pl.BlockSpec((1, tq, C_QW), lambda b, i: (b, i, qc0)),
                  pl.BlockSpec((1, t, C_KW), lambda b, i: (b, 0, kc0)),
                  pl.BlockSpec((1, t, C_KW), lambda b, i: (b, 0, kc0 + 1))],
        out_specs=pl.BlockSpec((1, tq, C_QW), lambda b, i: (b, i, 0)),
        out_shape=jax.ShapeDtypeStruct((bsz, t, C_QW), BF16),
        compiler_params=pltpu.CompilerParams(dimension_semantics=("parallel", "arbitrary"),
                                             vmem_limit_bytes=VMEM_LIMIT),
        name="attn_window",
    )(sink_lanes, proj, proj, proj)


def _layer_norm(y, g, b):
    mu = jnp.mean(y, axis=-1, keepdims=True)
    yc = y - mu
    var = jnp.mean(yc * yc, axis=-1, keepdims=True)
    return yc * lax.rsqrt(var + LN_EPS) * g + b


def _tail_kernel(oa_ref, ob_ref, oc_ref, x_ref, mod_ref, wo_ref, wgu_ref, wd_ref, lng_ref, lnb_ref, o_ref,
                 *, alpha, hidden, ck):
    mix = (jnp.dot(oa_ref[0], wo_ref[0:A_W, :], preferred_element_type=F32)
           + jnp.dot(ob_ref[0], wo_ref[A_W:A_W + B_W, :], preferred_element_type=F32)
           + jnp.dot(oc_ref[0], wo_ref[A_W + B_W:, :], preferred_element_type=F32))
    g1 = mod_ref[0, 2:3, :]
    x1 = _layer_norm(alpha * x_ref[0] + (1.0 + g1) * mix, lng_ref[0:1, :], lnb_ref[0:1, :])
    sh2 = mod_ref[0, 3:4, :]
    sc2 = mod_ref[0, 4:5, :]
    g2 = mod_ref[0, 5:6, :]
    h2 = (x1 * (1.0 + sc2) + sh2).astype(BF16)
    acc = jnp.zeros(x1.shape, F32)
    for c0 in range(0, hidden, ck):
        gate = jnp.dot(h2, wgu_ref[:, c0:c0 + ck], preferred_element_type=F32)
        up = jnp.dot(h2, wgu_ref[:, hidden + c0:hidden + c0 + ck], preferred_element_type=F32)
        act = (gate * jax.nn.sigmoid(gate) * up).astype(BF16)
        acc = acc + jnp.dot(act, wd_ref[c0:c0 + ck, :], preferred_element_type=F32)
    o_ref[0] = _layer_norm(alpha * x1 + (1.0 + g2) * acc, lng_ref[1:2, :], lnb_ref[1:2, :])


def _tail(oa, ob, oc, x, mod6, wo_bf, wgu_bf, wd_bf, ln_g, ln_b, alpha, tm=512):
    bsz, t, d = x.shape
    hidden = wd_bf.shape[0]
    kern = functools.partial(_tail_kernel, alpha=alpha, hidden=hidden, ck=256)
    row = lambda w: pl.BlockSpec((1, tm, w), lambda b, i: (b, i, 0))
    whole = lambda shp: pl.BlockSpec(shp, lambda b, i: (0, 0), pipeline_mode=pl.Buffered(1))
    return pl.pallas_call(
        kern,
        grid=(bsz, t // tm),
        in_specs=[row(A_W), row(B_W), row(C_QW), row(d),
                  pl.BlockSpec((1, 6, d), lambda b, i: (b, 0, 0)),
                  whole(wo_bf.shape), whole(wgu_bf.shape), whole(wd_bf.shape),
                  pl.BlockSpec((2, d), lambda b, i: (0, 0)),
                  pl.BlockSpec((2, d), lambda b, i: (0, 0))],
        out_specs=row(d),
        out_shape=jax.ShapeDtypeStruct((bsz, t, d), F32),
        compiler_params=pltpu.CompilerParams(dimension_semantics=("parallel", "arbitrary"),
                                             vmem_limit_bytes=VMEM_LIMIT),
        name="outproj_ffn",
    )(oa, ob, oc, x, mod6, wo_bf, wgu_bf, wd_bf, ln_g, ln_b)


def _q_col_scale():
    cs = np.ones((1, IN_WIDTH), np.float32)
    cs[0, 0:A_W] = A_QK_DIM ** -0.5 * LOG2E
    cs[0, 3 * A_W:3 * A_W + B_W] = HEAD_DIM ** -0.5 * LOG2E
    cs[0, 3 * A_W + 3 * B_W:3 * A_W + 3 * B_W + C_QW] = HEAD_DIM ** -0.5 * LOG2E
    return jnp.asarray(cs)


def kernel(x, c, w_ada, b_ada, w_in, lam, subln_g, sink, w_out, ln_g, ln_b, w_gu, w_down):
    depth = w_ada.shape[0]
    bsz, _, d = x.shape
    alpha = (2 * depth) ** 0.25
    mod = _modulation(c, w_ada, b_ada).reshape(depth, bsz, 6, d)
    col_scale = _q_col_scale()
    for layer in range(depth):
        lambda_init = 0.8 - 0.6 * math.exp(-0.3 * layer)
        proj = _in_projection(x, mod[layer], w_in[layer].astype(BF16), col_scale)
        g2 = jnp.tile(subln_g[layer], 2).reshape(1, LANES)
        oa = _attention_a(proj, lam[layer], g2, lambda_init)
        ob = _attention_b(proj)
        oc = _attention_c(proj, jnp.repeat(sink[layer], HEAD_DIM).reshape(1, C_QW))
        x = _tail(oa, ob, oc, x, mod[layer], w_out[layer].astype(BF16), w_gu[layer].astype(BF16),
                  w_down[layer].astype(BF16), ln_g[layer], ln_b[layer], alpha)
    return x
```

```python
import functools
import math

import numpy as np
import jax
import jax.numpy as jnp
from jax import lax
from jax.experimental import pallas as pl
from jax.experimental.pallas import tpu as pltpu

F32 = jnp.float32
BF16 = jnp.bfloat16

HEAD_DIM = 64
A_HEADS = 4
A_QK_DIM = 32
B_HEADS = 6
B_DILATIONS = (16, 4, 1)
B_HALF = 64
C_HEADS = 6
C_KV_HEADS = 2
C_HALF_WINDOW = 128
N_ATTN_HEADS = 16
LN_EPS = 1e-5
NEG = -1e30
LOG2E = math.log2(math.e)

LANES = 128
A_W = A_HEADS * HEAD_DIM
B_W = B_HEADS * HEAD_DIM
C_QW = C_HEADS * HEAD_DIM
C_KW = C_KV_HEADS * HEAD_DIM
IN_WIDTH = 3 * A_W + 3 * B_W + C_QW + 2 * C_KW
VMEM_LIMIT = 56 * 1024 * 1024

_S = 2.0 ** (-8.0 * np.arange(1, N_ATTN_HEADS + 1) / N_ATTN_HEADS)
SLOPES_C = [float(v) for v in _S[:C_HEADS]]
SLOPES_A = [float(v) for v in _S[C_HEADS:C_HEADS + A_HEADS]]
SLOPES_B = [float(v) for v in _S[C_HEADS + A_HEADS:]]


def _nt_dot(a, b):
    return lax.dot_general(a, b, (((1,), (1,)), ((), ())), preferred_element_type=F32)


def _lane_iota(shape):
    return lax.broadcasted_iota(jnp.int32, shape, len(shape) - 1)


def _mod_kernel(c_ref, w_ref, b_ref, o_ref):
    c = c_ref[...]
    a = (c * jax.nn.sigmoid(c)).astype(BF16)
    o_ref[0] = jnp.dot(a, w_ref[0].astype(BF16), preferred_element_type=F32) + b_ref[0]


def _modulation(c, w_ada, b_ada):
    depth, d, n = w_ada.shape
    bsz = c.shape[0]
    tn = 1536
    return pl.pallas_call(
        _mod_kernel,
        grid=(depth, n // tn),
        in_specs=[pl.BlockSpec((bsz, d), lambda l, j: (0, 0)),
                  pl.BlockSpec((1, d, tn), lambda l, j: (l, 0, j)),
                  pl.BlockSpec((1, 1, tn), lambda l, j: (l, 0, j))],
        out_specs=pl.BlockSpec((1, bsz, tn), lambda l, j: (l, 0, j)),
        out_shape=jax.ShapeDtypeStruct((depth, bsz, n), F32),
        compiler_params=pltpu.CompilerParams(dimension_semantics=("arbitrary", "arbitrary"),
                                             vmem_limit_bytes=VMEM_LIMIT),
        name="adaln_mod",
    )(c, w_ada, b_ada.reshape(depth, 1, n))


def _proj_kernel(x_ref, mod_ref, w_ref, cs_ref, o_ref, *, tn):
    sh = mod_ref[0, 0:1, :]
    sc = mod_ref[0, 1:2, :]
    h = (x_ref[0] * (1.0 + sc) + sh).astype(BF16)
    for n0 in range(0, o_ref.shape[2], tn):
        acc = jnp.dot(h, w_ref[:, n0:n0 + tn], preferred_element_type=F32)
        o_ref[0, :, n0:n0 + tn] = (acc * cs_ref[:, n0:n0 + tn]).astype(BF16)


def _in_projection(x, mod6, w_in_bf, col_scale, tm=512):
    bsz, t, d = x.shape
    n = w_in_bf.shape[1]
    return pl.pallas_call(
        functools.partial(_proj_kernel, tn=512),
        grid=(bsz, t // tm),
        in_specs=[pl.BlockSpec((1, tm, d), lambda b, i: (b, i, 0)),
                  pl.BlockSpec((1, 6, d), lambda b, i: (b, 0, 0)),
                  pl.BlockSpec((d, n), lambda b, i: (0, 0), pipeline_mode=pl.Buffered(1)),
                  pl.BlockSpec((1, n), lambda b, i: (0, 0))],
        out_specs=pl.BlockSpec((1, tm, n), lambda b, i: (b, i, 0)),
        out_shape=jax.ShapeDtypeStruct((bsz, t, n), BF16),
        compiler_params=pltpu.CompilerParams(dimension_semantics=("parallel", "arbitrary"),
                                             vmem_limit_bytes=VMEM_LIMIT),
        name="in_proj",
    )(x, mod6, w_in_bf, col_scale)


def _attn_a_tables(tq, tk):
    rel = (np.arange(tq)[:, None] - np.arange(tk)[None, :]).astype(np.float64)
    tabs = [np.stack([-s * LOG2E * np.abs(rel + c * tq) for c in range(tk // tq)]) for s in SLOPES_A]
    return jnp.asarray(np.stack(tabs).astype(np.float32))


def _alibi_lane(head):
    return 3 * A_QK_DIM if head % 2 == 0 else 0


def _attn_a_alibi_lanes(seq):
    pos = np.arange(seq, dtype=np.float64)
    bf = jnp.bfloat16
    kside = np.zeros((A_HEADS, seq, LANES), np.float32)
    qside = np.zeros((A_HEADS, seq, LANES), np.float32)
    for h, s in enumerate(SLOPES_A):
        a0 = _alibi_lane(h)
        x = (pos * (s * LOG2E)).astype(np.float32)
        parts, rest = [], x
        for _ in range(3):
            part = np.asarray(rest.astype(bf), np.float32)
            parts.append(part)
            rest = (rest - part).astype(np.float32)
        for i, part in enumerate(parts):
            kside[h, :, a0 + i] = part
            qside[h, :, a0 + 3 + i] = -part
        kside[h, :, a0 + 3:a0 + 6] = 1.0
        qside[h, :, a0:a0 + 3] = 1.0
    return jnp.asarray(kside, BF16), jnp.asarray(qside, BF16)


def _attn_a_kernel(lam_ref, g_ref, tab_ref, kside_ref, qside_ref, q_ref, k_ref, v_ref, o_ref,
                   kaug, vaug, qv_sc, t_even, t_odd, acc_sc, *, tq, tk, seq, lambda_init):
    qi = pl.program_id(1)
    r = tk // tq
    nkv = seq // tk
    kd = qi // r
    var_d = qi % r
    i0 = pl.multiple_of(qi * tq, tq)
    lane = _lane_iota((1, LANES))
    first = lane < HEAD_DIM
    nmap = 2 * A_HEADS
    nlt = tk // LANES
    rows_per_build = 512

    def group_lanes(gm):
        lo = A_QK_DIM * (gm % 4)
        return (lane >= lo) & (lane < lo + A_QK_DIM)

    @pl.when(qi == 0)
    def _build_kv_copies():
        for rc in range(seq // rows_per_build):
            rows = slice(rc * rows_per_build, (rc + 1) * rows_per_build)
            for p in range(A_HEADS // 2):
                kp = k_ref[0, rows, p * LANES:(p + 1) * LANES]
                vp = v_ref[0, rows, p * LANES:(p + 1) * LANES]
                for a in range(2):
                    vaug[2 * p + a, rows, :] = jnp.where(first if a == 0 else jnp.logical_not(first), vp,
                                                         jnp.ones_like(vp))
                for g in range(4):
                    gm = 4 * p + g
                    kaug[gm, rows, :] = jnp.where(group_lanes(gm), kp, kside_ref[gm // 2, rows, :])

    for p in range(A_HEADS // 2):
        qp = q_ref[0, :, p * LANES:(p + 1) * LANES]
        for g in range(4):
            gm = 4 * p + g
            side = qside_ref[gm // 2, pl.ds(i0, tq), :]
            qv_sc[gm, 0] = jnp.where(group_lanes(gm), qp, jnp.zeros_like(qp))
            qv_sc[gm, 1] = jnp.where(group_lanes(gm), qp, side)
            qv_sc[gm, 2] = jnp.where(group_lanes(gm), qp, -side)

    chunk_ids = [kd] + [(kd + cc) & (nkv - 1) for cc in range(1, nkv)]
    starts = [pl.multiple_of(cid * tk, tk) for cid in chunk_ids]
    q_variant = [0] + [jnp.where(cid < kd, 1, 2) for cid in chunk_ids[1:]]

    def logits_chunk(gm, cc, mx):
        t_sc = t_odd if gm % 2 else t_even
        s = _nt_dot(qv_sc[gm, q_variant[cc]], kaug[gm, pl.ds(starts[cc], tk), :])
        if cc == 0:
            s = s + tab_ref[gm // 2, var_d]
        t_sc[:, cc * tk:(cc + 1) * tk] = s
        cm = s[:, 0:LANES]
        for lt in range(1, nlt):
            cm = jnp.maximum(cm, s[:, lt * LANES:(lt + 1) * LANES])
        return cm if mx is None else jnp.maximum(mx, cm)

    def row_max(mx):
        return jnp.broadcast_to(jnp.max(mx, axis=1, keepdims=True), (tq, LANES))

    def pv_chunk(gm, cc, mb, acc):
        t_sc = t_odd if gm % 2 else t_even
        blocks = []
        for rb in range(0, tq, 64):
            blocks.append(jnp.concatenate(
                [jnp.exp2(t_sc[rb:rb + 64, cc * tk + lt * LANES:cc * tk + (lt + 1) * LANES]
                          - mb[rb:rb + 64, :]).astype(BF16) for lt in range(nlt)], axis=1))
        pv = jnp.dot(jnp.concatenate(blocks, axis=0), vaug[gm // 2, pl.ds(starts[cc], tk), :],
                     preferred_element_type=F32)
        return pv if acc is None else acc + pv

    mx = None
    for cc in range(nkv):
        mx = logits_chunk(0, cc, mx)
    mb = row_max(mx)
    for gm in range(nmap):
        mx, acc = None, None
        for cc in range(nkv):
            if gm + 1 < nmap:
                mx = logits_chunk(gm + 1, cc, mx)
            acc = pv_chunk(gm, cc, mb, acc)
        acc_sc[gm] = acc
        if gm + 1 < nmap:
            mb = row_max(mx)

    lf = lam_ref[...]
    lam_full = (jnp.exp(jnp.sum(lf[0:1] * lf[1:2], axis=(0, 1), keepdims=True))
                - jnp.exp(jnp.sum(lf[2:3] * lf[3:4], axis=(0, 1), keepdims=True)) + lambda_init)
    for p in range(A_HEADS // 2):
        d_heads = []
        for a in range(2):
            o_m = []
            for m2 in range(2):
                acc = acc_sc[4 * p + 2 * a + m2]
                o_m.append(acc / pltpu.roll(acc, HEAD_DIM, 1))
            d_heads.append(o_m[0] - lam_full * o_m[1])
        dp = jnp.where(first, d_heads[0], d_heads[1])
        sq = dp * dp
        ms0 = jnp.sum(jnp.where(first, sq, 0.0), axis=1, keepdims=True) * (1.0 / HEAD_DIM)
        ms1 = jnp.sum(jnp.where(first, 0.0, sq), axis=1, keepdims=True) * (1.0 / HEAD_DIM)
        ms = jnp.where(first, ms0, ms1)
        out = dp * lax.rsqrt(ms + LN_EPS) * g_ref[...] * (1.0 - lambda_init)
        o_ref[0, :, p * LANES:(p + 1) * LANES] = out.astype(o_ref.dtype)


def _attention_a(proj, lam_l, g2, lambda_init, tq=256, tk=512):
    bsz, t, _ = proj.shape
    tab = _attn_a_tables(tq, tk)
    kside, qside = _attn_a_alibi_lanes(t)
    nmap = 2 * A_HEADS
    kern = functools.partial(_attn_a_kernel, tq=tq, tk=tk, seq=t, lambda_init=lambda_init)
    return pl.pallas_call(
        kern,
        grid=(bsz, t // tq),
        in_specs=[pl.BlockSpec((4, A_QK_DIM), lambda b, i: (0, 0)),
                  pl.BlockSpec((1, LANES), lambda b, i: (0, 0)),
                  pl.BlockSpec(tab.shape, lambda b, i: (0, 0, 0, 0), pipeline_mode=pl.Buffered(1)),
                  pl.BlockSpec(kside.shape, lambda b, i: (0, 0, 0), pipeline_mode=pl.Buffered(1)),
                  pl.BlockSpec(qside.shape, lambda b, i: (0, 0, 0), pipeline_mode=pl.Buffered(1)),
                  pl.BlockSpec((1, tq, A_W), lambda b, i: (b, i, 0)),
                  pl.BlockSpec((1, t, A_W), lambda b, i: (b, 0, 1)),
                  pl.BlockSpec((1, t, A_W), lambda b, i: (b, 0, 2))],
        out_specs=pl.BlockSpec((1, tq, A_W), lambda b, i: (b, i, 0)),
        out_shape=jax.ShapeDtypeStruct((bsz, t, A_W), BF16),
        scratch_shapes=[pltpu.VMEM((nmap, t, LANES), BF16),
                        pltpu.VMEM((A_HEADS, t, LANES), BF16),
                        pltpu.VMEM((nmap, 3, tq, LANES), BF16),
                        pltpu.VMEM((tq, t), F32),
                        pltpu.VMEM((tq, t), F32),
                        pltpu.VMEM((nmap, tq, LANES), F32)],
        compiler_params=pltpu.CompilerParams(dimension_semantics=("parallel", "arbitrary"),
                                             vmem_limit_bytes=VMEM_LIMIT),
        name="attn_diff",
    )(lam_l, g2, tab, kside, qside, proj, proj, proj)


def _attn_b_kernel(sl_ref, q_ref, k_ref, v_ref, o_ref, qf, kf, vf, q4, k4, v4, m_st, l_st, o_st, bias_sc,
                   *, seq, tq, win, unroll):
    lane = _lane_iota((1, LANES))
    first = lane < HEAD_DIM
    halves = (first, jnp.logical_not(first))
    chunk = 512

    def prep(c, carry):
        r0 = pl.multiple_of(c * chunk, chunk)
        rows = pl.ds(r0, chunk)
        qf[rows, :] = q_ref[0, rows, :].astype(F32)
        kf[rows, :] = k_ref[0, rows, :].astype(F32)
        vf[rows, :] = v_ref[0, rows, :].astype(F32)
        return carry

    lax.fori_loop(0, seq // chunk, prep, 0)

    len4 = seq // 4
    for src_ref, dst_ref in ((qf, q4), (kf, k4), (vf, v4)):
        for r4 in range(4):
            for c0 in range(0, len4, chunk):
                dst_ref[r4 * len4 + c0:r4 * len4 + c0 + chunk, :] = src_ref[pl.ds(r4 + 4 * c0, chunk, stride=4), :]

    relb = (lax.broadcasted_iota(jnp.int32, (tq, win), 0) - lax.broadcasted_iota(jnp.int32, (tq, win), 1))

    for d in B_DILATIONS:
        length = seq // d
        nblk = length // tq

        for case in range(3):
            adist = jnp.abs(relb + case * B_HALF)
            adf = adist.astype(F32)
            for a in range(2):
                sl = sl_ref[0, a:a + 1, 0:1] * float(d)
                bias_sc[case, a] = jnp.where(adist <= B_HALF, -sl * adf, NEG)

        def state_rows(res, j, size, d=d):
            return pl.ds(j, size) if d == 1 else pl.ds(res + j * d, size, stride=d)

        def data_rows(res, j, size, d=d):
            if d == 1:
                return pl.ds(j, size)
            if d == 4:
                return pl.ds(res * len4 + j, size)
            return pl.ds((res % 4) * len4 + res // 4 + j * (d // 4), size, stride=d // 4)

        q_src, k_src, v_src = (qf, kf, vf) if d == 1 else (q4, k4, v4)

        merge = d != B_DILATIONS[0]

        def body(it, carry, length=length, nblk=nblk, state_rows=state_rows, data_rows=data_rows,
                 q_src=q_src, k_src=k_src, v_src=v_src, merge=merge):
            rqs, cases, qs, ks, vs = [], [], [], [], []
            for u in range(unroll):
                idx = it * unroll + u
                res = idx // nblk
                j0 = (idx % nblk) * tq
                start = jnp.clip(j0 - B_HALF, 0, length - win)
                rqs.append(state_rows(res, j0, tq))
                cases.append((j0 - start) // B_HALF)
                qs.append(q_src[data_rows(res, j0, tq), :])
                ks.append(k_src[data_rows(res, start, win), :].astype(BF16))
                vs.append(v_src[data_rows(res, start, win), :])
            items = [(u, a) for u in range(unroll) for a in range(2)]
            ss = [_nt_dot(jnp.where(halves[a], qs[u], 0.0).astype(BF16), ks[u]) for u, a in items]
            ts = [ss[i] + bias_sc[cases[u], a] for i, (u, a) in enumerate(items)]
            mbs = [jnp.max(t, axis=1, keepdims=True) for t in ts]
            prs = [jnp.exp2(t - mb).astype(BF16) for t, mb in zip(ts, mbs)]
            pvs = [jnp.dot(prs[i], jnp.where(halves[a], vs[u], 1.0).astype(BF16), preferred_element_type=F32)
                   for i, (u, a) in enumerate(items)]
            if merge:
                m_prev = [m_st[rq, :] for rq in rqs]
                l_prev = [l_st[rq, :] for rq in rqs]
                o_prev = [o_st[rq, :] for rq in rqs]
            for u in range(unroll):
                pv0, pv1 = pvs[2 * u], pvs[2 * u + 1]
                o_blk = jnp.where(first, pv0, pv1)
                l_blk = pltpu.roll(jnp.where(first, pv1, pv0), HEAD_DIM, 1)
                m_blk = jnp.where(first, mbs[2 * u], mbs[2 * u + 1])
                if not merge:
                    l_st[rqs[u], :] = l_blk
                    o_st[rqs[u], :] = o_blk
                    m_st[rqs[u], :] = jnp.broadcast_to(m_blk, (tq, LANES))
                    continue
                m_new = jnp.maximum(m_prev[u], m_blk)
                a_prev = jnp.exp2(m_prev[u] - m_new)
                a_blk = jnp.exp2(m_blk - m_new)
                l_st[rqs[u], :] = a_prev * l_prev[u] + a_blk * l_blk
                o_st[rqs[u], :] = a_prev * o_prev[u] + a_blk * o_blk
                m_st[rqs[u], :] = m_new
            return carry

        lax.fori_loop(0, seq // tq // unroll, body, 0)

    def fin(c, carry):
        r0 = pl.multiple_of(c * chunk, chunk)
        rows = pl.ds(r0, chunk)
        o_ref[0, rows, :] = (o_st[rows, :] / l_st[rows, :]).astype(o_ref.dtype)
        return carry

    lax.fori_loop(0, seq // chunk, fin, 0)


def _attention_b(proj, tq=128, win=256, unroll=4):
    bsz, t, _ = proj.shape
    npair = B_HEADS // 2
    sl = np.zeros((npair, 8, LANES), np.float32)
    for h, s in enumerate(SLOPES_B):
        sl[h // 2, h % 2, :] = s * LOG2E
    qb0 = 3 * A_W // LANES
    kern = functools.partial(_attn_b_kernel, seq=t, tq=tq, win=win, unroll=unroll)
    seq_spec = lambda off: pl.BlockSpec((1, t, LANES), lambda b, p, off=off: (b, 0, off + p))
    return pl.pallas_call(
        kern,
        grid=(bsz, npair),
        in_specs=[pl.BlockSpec((1, 8, LANES), lambda b, p: (p, 0, 0)),
                  seq_spec(qb0), seq_spec(qb0 + B_W // LANES), seq_spec(qb0 + 2 * B_W // LANES)],
        out_specs=pl.BlockSpec((1, t, LANES), lambda b, p: (b, 0, p)),
        out_shape=jax.ShapeDtypeStruct((bsz, t, B_W), BF16),
        scratch_shapes=[pltpu.VMEM((t, LANES), F32) for _ in range(9)] + [pltpu.VMEM((3, 2, tq, win), F32)],
        compiler_params=pltpu.CompilerParams(dimension_semantics=("parallel", "arbitrary"),
                                             vmem_limit_bytes=VMEM_LIMIT),
        name="attn_dilated",
    )(jnp.asarray(sl), proj, proj, proj)


def _attn_c_tables(tq, win):
    rel = np.arange(tq)[:, None] - np.arange(win)[None, :]
    tabs = []
    for case in range(3):
        dist = np.abs(rel + case * C_HALF_WINDOW).astype(np.float64)
        tabs.append(np.stack([np.where(dist <= C_HALF_WINDOW, -s * LOG2E * dist, NEG) for s in SLOPES_C]))
    return jnp.asarray(np.stack(tabs).astype(np.float32))


def _attn_c_kernel(sink_ref, tab_ref, q_ref, k_ref, v_ref, o_ref, *, seq, tq, win):
    qi = pl.program_id(1)
    i0 = qi * tq
    start = pl.multiple_of(jnp.clip(i0 - C_HALF_WINDOW, 0, seq - win), C_HALF_WINDOW)
    case = (i0 - start) // C_HALF_WINDOW
    k2 = k_ref[0, pl.ds(start, win), :].astype(F32)
    v2 = v_ref[0, pl.ds(start, win), :]
    k_same = k2.astype(BF16)
    k_swap = pltpu.roll(k2, HEAD_DIM, 1).astype(BF16)
    lane = _lane_iota((1, LANES))
    first = lane < HEAD_DIM
    halves = (first, jnp.logical_not(first))
    v_aug = [jnp.where(halves[g], v2, jnp.ones_like(v2)) for g in range(C_KV_HEADS)]
    rep = C_HEADS // C_KV_HEADS
    heads = range(C_HEADS)
    kv_of = [h // rep for h in heads]
    qps = [q_ref[0, :, p * LANES:(p + 1) * LANES] for p in range(C_HEADS // 2)]
    qms = [jnp.where(halves[h % 2], qps[h // 2], jnp.zeros_like(qps[h // 2])) for h in heads]
    ss = [_nt_dot(qms[h], k_same if h % 2 == kv_of[h] else k_swap) for h in heads]
    ts = [ss[h] + tab_ref[case, h] for h in heads]
    ms = [jnp.max(t, axis=1, keepdims=True) for t in ts]
    prs = [jnp.exp2(ts[h] - ms[h]).astype(BF16) for h in heads]
    pvs = [jnp.dot(prs[h], v_aug[kv_of[h]], preferred_element_type=F32) for h in heads]
    swp = [pltpu.roll(pv, HEAD_DIM, 1) for pv in pvs]
    outs = [pvs[h] if h % 2 == kv_of[h] else swp[h] for h in heads]
    dens = [swp[h] if h % 2 == kv_of[h] else pvs[h] for h in heads]
    for p in range(C_HEADS // 2):
        h0, h1 = 2 * p, 2 * p + 1
        m_pair = jnp.where(first, ms[h0], ms[h1])
        l_pair = jnp.where(first, dens[h0], dens[h1])
        o_pair = jnp.where(first, outs[h0], outs[h1])
        sk = sink_ref[:, p * LANES:(p + 1) * LANES] * LOG2E
        mx = jnp.maximum(m_pair, sk)
        a_ = jnp.exp2(m_pair - mx)
        out = o_pair * a_ / (l_pair * a_ + jnp.exp2(sk - mx))
        o_ref[0, :, p * LANES:(p + 1) * LANES] = out.astype(o_ref.dtype)


def _attention_c(proj, sink_lanes, tq=128, win=384):
    bsz, t, _ = proj.shape
    tab = _attn_c_tables(tq, win)
    kern = functools.partial(_attn_c_kernel, seq=t, tq=tq, win=win)
    qc0 = (3 * A_W + 3 * B_W) // C_QW
    kc0 = (3 * A_W + 3 * B_W + C_QW) // C_KW
    return pl.pallas_call(
        kern,
        grid=(bsz, t // tq),
        in_specs=[pl.BlockSpec((1, C_QW), lambda b, i: (0, 0)),
                  pl.BlockSpec(tab.shape, lambda b, i: (0, 0, 0, 0), pipeline_mode=pl.Buffered(1)),
                  pl.BlockSpec((1, tq, C_QW), lambda b, i: (b, i, qc0)),
                  pl.BlockSpec((1, t, C_KW), lambda b, i: (b, 0, kc0)),
                  pl.BlockSpec((1, t, C_KW), lambda b, i: (b, 0, kc0 + 1))],
        out_specs=pl.BlockSpec((1, tq, C_QW), lambda b, i: (b, i, 0)),
        out_shape=jax.ShapeDtypeStruct((bsz, t, C_QW), BF16),
        compiler_params=pltpu.CompilerParams(dimension_semantics=("parallel", "arbitrary"),
                                             vmem_limit_bytes=VMEM_LIMIT),
        name="attn_window",
    )(sink_lanes, tab, proj, proj, proj)


def _layer_norm(y, g, b):
    mu = jnp.mean(y, axis=-1, keepdims=True)
    yc = y - mu
    var = jnp.mean(yc * yc, axis=-1, keepdims=True)
    return yc * lax.rsqrt(var + LN_EPS) * g + b


def _tail_kernel(oa_ref, ob_ref, oc_ref, x_ref, mod_ref, wo_ref, wgu_ref, wd_ref, lng_ref, lnb_ref, o_ref,
                 *, alpha, hidden, ck):
    mix = (jnp.dot(oa_ref[0], wo_ref[0:A_W, :], preferred_element_type=F32)
           + jnp.dot(ob_ref[0], wo_ref[A_W:A_W + B_W, :], preferred_element_type=F32)
           + jnp.dot(oc_ref[0], wo_ref[A_W + B_W:, :], preferred_element_type=F32))
    g1 = mod_ref[0, 2:3, :]
    x1 = _layer_norm(alpha * x_ref[0] + (1.0 + g1) * mix, lng_ref[0:1, :], lnb_ref[0:1, :])
    sh2 = mod_ref[0, 3:4, :]
    sc2 = mod_ref[0, 4:5, :]
    g2 = mod_ref[0, 5:6, :]
    h2 = (x1 * (1.0 + sc2) + sh2).astype(BF16)
    acc = jnp.zeros(x1.shape, F32)
    for c0 in range(0, hidden, ck):
        gate = jnp.dot(h2, wgu_ref[:, c0:c0 + ck], preferred_element_type=F32)
        up = jnp.dot(h2, wgu_ref[:, hidden + c0:hidden + c0 + ck], preferred_element_type=F32)
        act = (gate * jax.nn.sigmoid(gate) * up).astype(BF16)
        acc = acc + jnp.dot(act, wd_ref[c0:c0 + ck, :], preferred_element_type=F32)
    o_ref[0] = _layer_norm(alpha * x1 + (1.0 + g2) * acc, lng_ref[1:2, :], lnb_ref[1:2, :])


def _tail(oa, ob, oc, x, mod6, wo_bf, wgu_bf, wd_bf, ln_g, ln_b, alpha, tm=512):
    bsz, t, d = x.shape
    hidden = wd_bf.shape[0]
    kern = functools.partial(_tail_kernel, alpha=alpha, hidden=hidden, ck=256)
    row = lambda w: pl.BlockSpec((1, tm, w), lambda b, i: (b, i, 0))
    whole = lambda shp: pl.BlockSpec(shp, lambda b, i: (0, 0), pipeline_mode=pl.Buffered(1))
    return pl.pallas_call(
        kern,
        grid=(bsz, t // tm),
        in_specs=[row(A_W), row(B_W), row(C_QW), row(d),
                  pl.BlockSpec((1, 6, d), lambda b, i: (b, 0, 0)),
                  whole(wo_bf.shape), whole(wgu_bf.shape), whole(wd_bf.shape),
                  pl.BlockSpec((2, d), lambda b, i: (0, 0)),
                  pl.BlockSpec((2, d), lambda b, i: (0, 0))],
        out_specs=row(d),
        out_shape=jax.ShapeDtypeStruct((bsz, t, d), F32),
        compiler_params=pltpu.CompilerParams(dimension_semantics=("parallel", "arbitrary"),
                                             vmem_limit_bytes=VMEM_LIMIT),
        name="outproj_ffn",
    )(oa, ob, oc, x, mod6, wo_bf, wgu_bf, wd_bf, ln_g, ln_b)


def _q_col_scale():
    cs = np.ones((1, IN_WIDTH), np.float32)
    cs[0, 0:A_W] = A_QK_DIM ** -0.5 * LOG2E
    cs[0, 3 * A_W:3 * A_W + B_W] = HEAD_DIM ** -0.5 * LOG2E
    cs[0, 3 * A_W + 3 * B_W:3 * A_W + 3 * B_W + C_QW] = HEAD_DIM ** -0.5 * LOG2E
    return jnp.asarray(cs)


def kernel(x, c, w_ada, b_ada, w_in, lam, subln_g, sink, w_out, ln_g, ln_b, w_gu, w_down):
    depth = w_ada.shape[0]
    bsz, _, d = x.shape
    alpha = (2 * depth) ** 0.25
    mod = _modulation(c, w_ada, b_ada).reshape(depth, bsz, 6, d)
    col_scale = _q_col_scale()
    for layer in range(depth):
        lambda_init = 0.8 - 0.6 * math.exp(-0.3 * layer)
        proj = _in_projection(x, mod[layer], w_in[layer].astype(BF16), col_scale)
        g2 = jnp.tile(subln_g[layer], 2).reshape(1, LANES)
        oa = _attention_a(proj, lam[layer], g2, lambda_init)
        ob = _attention_b(proj)
        oc = _attention_c(proj, jnp.repeat(sink[layer], HEAD_DIM).reshape(1, C_QW))
        x = _tail(oa, ob, oc, x, mod[layer], w_out[layer].astype(BF16), w_gu[layer].astype(BF16),
                  w_down[layer].astype(BF16), ln_g[layer], ln_b[layer], alpha)
    return x
```

```python
import functools
import math

import numpy as np
import jax
import jax.numpy as jnp
from jax import lax
from jax.experimental import pallas as pl
from jax.experimental.pallas import tpu as pltpu

F32 = jnp.float32
BF16 = jnp.bfloat16

HEAD_DIM = 64
A_HEADS = 4
A_QK_DIM = 32
B_HEADS = 6
B_DILATIONS = (16, 4, 1)
B_HALF = 64
C_HEADS = 6
C_KV_HEADS = 2
C_HALF_WINDOW = 128
N_ATTN_HEADS = 16
LN_EPS = 1e-5
NEG = -1e30
LOG2E = math.log2(math.e)

LANES = 128
A_W = A_HEADS * HEAD_DIM
B_W = B_HEADS * HEAD_DIM
C_QW = C_HEADS * HEAD_DIM
C_KW = C_KV_HEADS * HEAD_DIM
IN_WIDTH = 3 * A_W + 3 * B_W + C_QW + 2 * C_KW
VMEM_LIMIT = 56 * 1024 * 1024

_S = 2.0 ** (-8.0 * np.arange(1, N_ATTN_HEADS + 1) / N_ATTN_HEADS)
SLOPES_C = [float(v) for v in _S[:C_HEADS]]
SLOPES_A = [float(v) for v in _S[C_HEADS:C_HEADS + A_HEADS]]
SLOPES_B = [float(v) for v in _S[C_HEADS + A_HEADS:]]


def _nt_dot(a, b):
    return lax.dot_general(a, b, (((1,), (1,)), ((), ())), preferred_element_type=F32)


def _lane_iota(shape):
    return lax.broadcasted_iota(jnp.int32, shape, len(shape) - 1)


def _mod_kernel(c_ref, w_ref, b_ref, o_ref):
    c = c_ref[...]
    a = (c * jax.nn.sigmoid(c)).astype(BF16)
    o_ref[0] = jnp.dot(a, w_ref[0].astype(BF16), preferred_element_type=F32) + b_ref[0]


def _modulation(c, w_ada, b_ada):
    depth, d, n = w_ada.shape
    bsz = c.shape[0]
    tn = 1536
    return pl.pallas_call(
        _mod_kernel,
        grid=(depth, n // tn),
        in_specs=[pl.BlockSpec((bsz, d), lambda l, j: (0, 0)),
                  pl.BlockSpec((1, d, tn), lambda l, j: (l, 0, j)),
                  pl.BlockSpec((1, 1, tn), lambda l, j: (l, 0, j))],
        out_specs=pl.BlockSpec((1, bsz, tn), lambda l, j: (l, 0, j)),
        out_shape=jax.ShapeDtypeStruct((depth, bsz, n), F32),
        compiler_params=pltpu.CompilerParams(dimension_semantics=("arbitrary", "arbitrary"),
                                             vmem_limit_bytes=VMEM_LIMIT),
        name="adaln_mod",
    )(c, w_ada, b_ada.reshape(depth, 1, n))


def _proj_kernel(x_ref, mod_ref, w_ref, cs_ref, o_ref, *, tn):
    sh = mod_ref[0, 0:1, :]
    sc = mod_ref[0, 1:2, :]
    h = (x_ref[0] * (1.0 + sc) + sh).astype(BF16)
    for n0 in range(0, o_ref.shape[2], tn):
        acc = jnp.dot(h, w_ref[:, n0:n0 + tn], preferred_element_type=F32)
        o_ref[0, :, n0:n0 + tn] = (acc * cs_ref[:, n0:n0 + tn]).astype(BF16)


def _in_projection(x, mod6, w_in_bf, col_scale, tm=512):
    bsz, t, d = x.shape
    n = w_in_bf.shape[1]
    return pl.pallas_call(
        functools.partial(_proj_kernel, tn=512),
        grid=(bsz, t // tm),
        in_specs=[pl.BlockSpec((1, tm, d), lambda b, i: (b, i, 0)),
                  pl.BlockSpec((1, 6, d), lambda b, i: (b, 0, 0)),
                  pl.BlockSpec((d, n), lambda b, i: (0, 0), pipeline_mode=pl.Buffered(1)),
                  pl.BlockSpec((1, n), lambda b, i: (0, 0))],
        out_specs=pl.BlockSpec((1, tm, n), lambda b, i: (b, i, 0)),
        out_shape=jax.ShapeDtypeStruct((bsz, t, n), BF16),
        compiler_params=pltpu.CompilerParams(dimension_semantics=("parallel", "arbitrary"),
                                             vmem_limit_bytes=VMEM_LIMIT),
        name="in_proj",
    )(x, mod6, w_in_bf, col_scale)


def _attn_a_tables(tq, tk):
    rel = (np.arange(tq)[:, None] - np.arange(tk)[None, :]).astype(np.float64)
    tabs = [np.stack([-s * LOG2E * np.abs(rel + c * tq) for c in range(tk // tq)]) for s in SLOPES_A]
    return jnp.asarray(np.stack(tabs).astype(np.float32))


def _alibi_lane(head):
    return 3 * A_QK_DIM if head % 2 == 0 else 0


def _attn_a_alibi_lanes(seq):
    pos = np.arange(seq, dtype=np.float64)
    bf = jnp.bfloat16
    kside = np.zeros((A_HEADS, seq, LANES), np.float32)
    qside = np.zeros((A_HEADS, seq, LANES), np.float32)
    for h, s in enumerate(SLOPES_A):
        a0 = _alibi_lane(h)
        x = (pos * (s * LOG2E)).astype(np.float32)
        parts, rest = [], x
        for _ in range(3):
            part = np.asarray(rest.astype(bf), np.float32)
            parts.append(part)
            rest = (rest - part).astype(np.float32)
        for i, part in enumerate(parts):
            kside[h, :, a0 + i] = part
            qside[h, :, a0 + 3 + i] = -part
        kside[h, :, a0 + 3:a0 + 6] = 1.0
        qside[h, :, a0:a0 + 3] = 1.0
    return jnp.asarray(kside, BF16), jnp.asarray(qside, BF16)


def _attn_a_kernel(lam_ref, g_ref, tab_ref, kside_ref, qside_ref, q_ref, k_ref, v_ref, o_ref,
                   kaug, vaug, qv_sc, t_even, t_odd, acc_sc, *, tq, tk, seq, lambda_init):
    qi = pl.program_id(1)
    r = tk // tq
    nkv = seq // tk
    kd = qi // r
    var_d = qi % r
    i0 = pl.multiple_of(qi * tq, tq)
    lane = _lane_iota((1, LANES))
    first = lane < HEAD_DIM
    nmap = 2 * A_HEADS
    nlt = tk // LANES
    rows_per_build = 512

    def group_lanes(gm):
        lo = A_QK_DIM * (gm % 4)
        return (lane >= lo) & (lane < lo + A_QK_DIM)

    @pl.when(qi == 0)
    def _build_kv_copies():
        for rc in range(seq // rows_per_build):
            rows = slice(rc * rows_per_build, (rc + 1) * rows_per_build)
            for p in range(A_HEADS // 2):
                kp = k_ref[0, rows, p * LANES:(p + 1) * LANES]
                vp = v_ref[0, rows, p * LANES:(p + 1) * LANES]
                for a in range(2):
                    vaug[2 * p + a, rows, :] = jnp.where(first if a == 0 else jnp.logical_not(first), vp,
                                                         jnp.ones_like(vp))
                for g in range(4):
                    gm = 4 * p + g
                    kaug[gm, rows, :] = jnp.where(group_lanes(gm), kp, kside_ref[gm // 2, rows, :])

    for p in range(A_HEADS // 2):
        qp = q_ref[0, :, p * LANES:(p + 1) * LANES]
        for g in range(4):
            gm = 4 * p + g
            side = qside_ref[gm // 2, pl.ds(i0, tq), :]
            qv_sc[gm, 0] = jnp.where(group_lanes(gm), qp, jnp.zeros_like(qp))
            qv_sc[gm, 1] = jnp.where(group_lanes(gm), qp, side)
            qv_sc[gm, 2] = jnp.where(group_lanes(gm), qp, -side)

    chunk_ids = [kd] + [(kd + cc) & (nkv - 1) for cc in range(1, nkv)]
    starts = [pl.multiple_of(cid * tk, tk) for cid in chunk_ids]
    q_variant = [0] + [jnp.where(cid < kd, 1, 2) for cid in chunk_ids[1:]]

    def logits_chunk(gm, cc, mx):
        t_sc = t_odd if gm % 2 else t_even
        s = _nt_dot(qv_sc[gm, q_variant[cc]], kaug[gm, pl.ds(starts[cc], tk), :])
        if cc == 0:
            s = s + tab_ref[gm // 2, var_d]
        t_sc[:, cc * tk:(cc + 1) * tk] = s
        cm = s[:, 0:LANES]
        for lt in range(1, nlt):
            cm = jnp.maximum(cm, s[:, lt * LANES:(lt + 1) * LANES])
        return cm if mx is None else jnp.maximum(mx, cm)

    def row_max(mx):
        return jnp.broadcast_to(jnp.max(mx, axis=1, keepdims=True), (tq, LANES))

    def pv_chunk(gm, cc, mb, acc):
        t_sc = t_odd if gm % 2 else t_even
        blocks = []
        for rb in range(0, tq, 64):
            blocks.append(jnp.concatenate(
                [jnp.exp2(t_sc[rb:rb + 64, cc * tk + lt * LANES:cc * tk + (lt + 1) * LANES]
                          - mb[rb:rb + 64, :]).astype(BF16) for lt in range(nlt)], axis=1))
        pv = jnp.dot(jnp.concatenate(blocks, axis=0), vaug[gm // 2, pl.ds(starts[cc], tk), :],
                     preferred_element_type=F32)
        return pv if acc is None else acc + pv

    mx = None
    for cc in range(nkv):
        mx = logits_chunk(0, cc, mx)
    mb = row_max(mx)
    for gm in range(nmap):
        mx, acc = None, None
        for cc in range(nkv):
            if gm + 1 < nmap:
                mx = logits_chunk(gm + 1, cc, mx)
            acc = pv_chunk(gm, cc, mb, acc)
        acc_sc[gm] = acc
        if gm + 1 < nmap:
            mb = row_max(mx)

    lf = lam_ref[...]
    lam_full = (jnp.exp(jnp.sum(lf[0:1] * lf[1:2], axis=(0, 1), keepdims=True))
                - jnp.exp(jnp.sum(lf[2:3] * lf[3:4], axis=(0, 1), keepdims=True)) + lambda_init)
    for p in range(A_HEADS // 2):
        d_heads = []
        for a in range(2):
            o_m = []
            for m2 in range(2):
                acc = acc_sc[4 * p + 2 * a + m2]
                o_m.append(acc / pltpu.roll(acc, HEAD_DIM, 1))
            d_heads.append(o_m[0] - lam_full * o_m[1])
        dp = jnp.where(first, d_heads[0], d_heads[1])
        sq = dp * dp
        ms0 = jnp.sum(jnp.where(first, sq, 0.0), axis=1, keepdims=True) * (1.0 / HEAD_DIM)
        ms1 = jnp.sum(jnp.where(first, 0.0, sq), axis=1, keepdims=True) * (1.0 / HEAD_DIM)
        ms = jnp.where(first, ms0, ms1)
        out = dp * lax.rsqrt(ms + LN_EPS) * g_ref[...] * (1.0 - lambda_init)
        o_ref[0, :, p * LANES:(p + 1) * LANES] = out.astype(o_ref.dtype)


def _attention_a(proj, lam_l, g2, lambda_init, tq=256, tk=512):
    bsz, t, _ = proj.shape
    tab = _attn_a_tables(tq, tk)
    kside, qside = _attn_a_alibi_lanes(t)
    nmap = 2 * A_HEADS
    kern = functools.partial(_attn_a_kernel, tq=tq, tk=tk, seq=t, lambda_init=lambda_init)
    return pl.pallas_call(
        kern,
        grid=(bsz, t // tq),
        in_specs=[pl.BlockSpec((4, A_QK_DIM), lambda b, i: (0, 0)),
                  pl.BlockSpec((1, LANES), lambda b, i: (0, 0)),
                  pl.BlockSpec(tab.shape, lambda b, i: (0, 0, 0, 0), pipeline_mode=pl.Buffered(1)),
                  pl.BlockSpec(kside.shape, lambda b, i: (0, 0, 0), pipeline_mode=pl.Buffered(1)),
                  pl.BlockSpec(qside.shape, lambda b, i: (0, 0, 0), pipeline_mode=pl.Buffered(1)),
                  pl.BlockSpec((1, tq, A_W), lambda b, i: (b, i, 0)),
                  pl.BlockSpec((1, t, A_W), lambda b, i: (b, 0, 1)),
                  pl.BlockSpec((1, t, A_W), lambda b, i: (b, 0, 2))],
        out_specs=pl.BlockSpec((1, tq, A_W), lambda b, i: (b, i, 0)),
        out_shape=jax.ShapeDtypeStruct((bsz, t, A_W), BF16),
        scratch_shapes=[pltpu.VMEM((nmap, t, LANES), BF16),
                        pltpu.VMEM((A_HEADS, t, LANES), BF16),
                        pltpu.VMEM((nmap, 3, tq, LANES), BF16),
                        pltpu.VMEM((tq, t), F32),
                        pltpu.VMEM((tq, t), F32),
                        pltpu.VMEM((nmap, tq, LANES), F32)],
        compiler_params=pltpu.CompilerParams(dimension_semantics=("parallel", "arbitrary"),
                                             vmem_limit_bytes=VMEM_LIMIT),
        name="attn_diff",
    )(lam_l, g2, tab, kside, qside, proj, proj, proj)


def _attn_b_kernel(sl_ref, q_ref, k_ref, v_ref, o_ref, qf, kf, vf, q4, k4, v4, m_st, l_st, o_st, bias_sc,
                   *, seq, tq, win, unroll):
    lane = _lane_iota((1, LANES))
    first = lane < HEAD_DIM
    halves = (first, jnp.logical_not(first))
    chunk = 512

    def prep(c, carry):
        r0 = pl.multiple_of(c * chunk, chunk)
        rows = pl.ds(r0, chunk)
        qf[rows, :] = q_ref[0, rows, :].astype(F32)
        kf[rows, :] = k_ref[0, rows, :].astype(F32)
        vf[rows, :] = v_ref[0, rows, :].astype(F32)
        return carry

    lax.fori_loop(0, seq // chunk, prep, 0)

    len4 = seq // 4
    for src_ref, dst_ref in ((qf, q4), (kf, k4), (vf, v4)):
        for r4 in range(4):
            for c0 in range(0, len4, chunk):
                dst_ref[r4 * len4 + c0:r4 * len4 + c0 + chunk, :] = src_ref[pl.ds(r4 + 4 * c0, chunk, stride=4), :]

    relb = (lax.broadcasted_iota(jnp.int32, (tq, win), 0) - lax.broadcasted_iota(jnp.int32, (tq, win), 1))

    for d in B_DILATIONS:
        length = seq // d
        nblk = length // tq

        for case in range(3):
            adist = jnp.abs(relb + case * B_HALF)
            adf = adist.astype(F32)
            for a in range(2):
                sl = sl_ref[0, a:a + 1, 0:1] * float(d)
                bias_sc[case, a] = jnp.where(adist <= B_HALF, -sl * adf, NEG)

        def state_rows(res, j, size, d=d):
            return pl.ds(j, size) if d == 1 else pl.ds(res + j * d, size, stride=d)

        def data_rows(res, j, size, d=d):
            if d == 1:
                return pl.ds(j, size)
            if d == 4:
                return pl.ds(res * len4 + j, size)
            return pl.ds((res % 4) * len4 + res // 4 + j * (d // 4), size, stride=d // 4)

        q_src, k_src, v_src = (qf, kf, vf) if d == 1 else (q4, k4, v4)

        merge = d != B_DILATIONS[0]

        def body(it, carry, length=length, nblk=nblk, state_rows=state_rows, data_rows=data_rows,
                 q_src=q_src, k_src=k_src, v_src=v_src, merge=merge):
            rqs, cases, qs, ks, vs = [], [], [], [], []
            for u in range(unroll):
                idx = it * unroll + u
                res = idx // nblk
                j0 = (idx % nblk) * tq
                start = jnp.clip(j0 - B_HALF, 0, length - win)
                rqs.append(state_rows(res, j0, tq))
                cases.append((j0 - start) // B_HALF)
                qs.append(q_src[data_rows(res, j0, tq), :])
                ks.append(k_src[data_rows(res, start, win), :].astype(BF16))
                vs.append(v_src[data_rows(res, start, win), :])
            items = [(u, a) for u in range(unroll) for a in range(2)]
            ss = [_nt_dot(jnp.where(halves[a], qs[u], 0.0).astype(BF16), ks[u]) for u, a in items]
            ts = [ss[i] + bias_sc[cases[u], a] for i, (u, a) in enumerate(items)]
            mbs = [jnp.max(t, axis=1, keepdims=True) for t in ts]
            prs = [jnp.exp2(t - mb).astype(BF16) for t, mb in zip(ts, mbs)]
            pvs = [jnp.dot(prs[i], jnp.where(halves[a], vs[u], 1.0).astype(BF16), preferred_element_type=F32)
                   for i, (u, a) in enumerate(items)]
            if merge:
                m_prev = [m_st[rq, :] for rq in rqs]
                l_prev = [l_st[rq, :] for rq in rqs]
                o_prev = [o_st[rq, :] for rq in rqs]
            for u in range(unroll):
                pv0, pv1 = pvs[2 * u], pvs[2 * u + 1]
                o_blk = jnp.where(first, pv0, pv1)
                l_blk = pltpu.roll(jnp.where(first, pv1, pv0), HEAD_DIM, 1)
                m_blk = jnp.where(first, mbs[2 * u], mbs[2 * u + 1])
                if not merge:
                    l_st[rqs[u], :] = l_blk
                    o_st[rqs[u], :] = o_blk
                    m_st[rqs[u], :] = jnp.broadcast_to(m_blk, (tq, LANES))
                    continue
                m_new = jnp.maximum(m_prev[u], m_blk)
                a_prev = jnp.exp2(m_prev[u] - m_new)
                a_blk = jnp.exp2(m_blk - m_new)
                l_st[rqs[u], :] = a_prev * l_prev[u] + a_blk * l_blk
                o_st[rqs[u], :] = a_prev * o_prev[u] + a_blk * o_blk
                m_st[rqs[u], :] = m_new
            return carry

        lax.fori_loop(0, seq // tq // unroll, body, 0)

    def fin(c, carry):
        r0 = pl.multiple_of(c * chunk, chunk)
        rows = pl.ds(r0, chunk)
        o_ref[0, rows, :] = (o_st[rows, :] / l_st[rows, :]).astype(o_ref.dtype)
        return carry

    lax.fori_loop(0, seq // chunk, fin, 0)


def _attention_b(proj, tq=128, win=256, unroll=4):
    bsz, t, _ = proj.shape
    npair = B_HEADS // 2
    sl = np.zeros((npair, 8, LANES), np.float32)
    for h, s in enumerate(SLOPES_B):
        sl[h // 2, h % 2, :] = s * LOG2E
    qb0 = 3 * A_W // LANES
    kern = functools.partial(_attn_b_kernel, seq=t, tq=tq, win=win, unroll=unroll)
    seq_spec = lambda off: pl.BlockSpec((1, t, LANES), lambda b, p, off=off: (b, 0, off + p))
    return pl.pallas_call(
        kern,
        grid=(bsz, npair),
        in_specs=[pl.BlockSpec((1, 8, LANES), lambda b, p: (p, 0, 0)),
                  seq_spec(qb0), seq_spec(qb0 + B_W // LANES), seq_spec(qb0 + 2 * B_W // LANES)],
        out_specs=pl.BlockSpec((1, t, LANES), lambda b, p: (b, 0, p)),
        out_shape=jax.ShapeDtypeStruct((bsz, t, B_W), BF16),
        scratch_shapes=[pltpu.VMEM((t, LANES), F32) for _ in range(9)] + [pltpu.VMEM((3, 2, tq, win), F32)],
        compiler_params=pltpu.CompilerParams(dimension_semantics=("parallel", "arbitrary"),
                                             vmem_limit_bytes=VMEM_LIMIT),
        name="attn_dilated",
    )(jnp.asarray(sl), proj, proj, proj)


def _attn_c_tables(tq, win):
    rel = np.arange(tq)[:, None] - np.arange(win)[None, :]
    tabs = []
    for case in range(3):
        dist = np.abs(rel + case * C_HALF_WINDOW).astype(np.float64)
        tabs.append(np.stack([np.where(dist <= C_HALF_WINDOW, -s * LOG2E * dist, NEG) for s in SLOPES_C]))
    return jnp.asarray(np.stack(tabs).astype(np.float32))


def _attn_c_kernel(sink_ref, tab_ref, q_ref, k_ref, v_ref, o_ref, *, seq, tq, win, nsub):
    qi = pl.program_id(1)
    lane = _lane_iota((1, LANES))
    first = lane < HEAD_DIM
    halves = (first, jnp.logical_not(first))
    rep = C_HEADS // C_KV_HEADS
    kv_of = [h // rep for h in range(C_HEADS)]
    items = [(sb, h) for sb in range(nsub) for h in range(C_HEADS)]
    cases, k_same, k_swap, v_aug = [], [], [], []
    for sb in range(nsub):
        i0 = (qi * nsub + sb) * tq
        start = pl.multiple_of(jnp.clip(i0 - C_HALF_WINDOW, 0, seq - win), C_HALF_WINDOW)
        cases.append((i0 - start) // C_HALF_WINDOW)
        k2 = k_ref[0, pl.ds(start, win), :].astype(F32)
        v2 = v_ref[0, pl.ds(start, win), :]
        k_same.append(k2.astype(BF16))
        k_swap.append(pltpu.roll(k2, HEAD_DIM, 1).astype(BF16))
        v_aug.append([jnp.where(halves[g], v2, jnp.ones_like(v2)) for g in range(C_KV_HEADS)])

    def q_masked(sb, h):
        qp = q_ref[0, sb * tq:(sb + 1) * tq, (h // 2) * LANES:(h // 2 + 1) * LANES]
        return jnp.where(halves[h % 2], qp, jnp.zeros_like(qp))

    ss = [_nt_dot(q_masked(sb, h), k_same[sb] if h % 2 == kv_of[h] else k_swap[sb]) for sb, h in items]
    ts = [ss[i] + tab_ref[cases[sb], h] for i, (sb, h) in enumerate(items)]
    ms = [jnp.max(t, axis=1, keepdims=True) for t in ts]
    prs = [jnp.exp2(t - m).astype(BF16) for t, m in zip(ts, ms)]
    pvs = [jnp.dot(prs[i], v_aug[sb][kv_of[h]], preferred_element_type=F32) for i, (sb, h) in enumerate(items)]
    swp = [pltpu.roll(pv, HEAD_DIM, 1) for pv in pvs]
    outs = [pvs[i] if h % 2 == kv_of[h] else swp[i] for i, (sb, h) in enumerate(items)]
    dens = [swp[i] if h % 2 == kv_of[h] else pvs[i] for i, (sb, h) in enumerate(items)]
    for sb in range(nsub):
        for p in range(C_HEADS // 2):
            i0, i1 = sb * C_HEADS + 2 * p, sb * C_HEADS + 2 * p + 1
            m_pair = jnp.where(first, ms[i0], ms[i1])
            l_pair = jnp.where(first, dens[i0], dens[i1])
            o_pair = jnp.where(first, outs[i0], outs[i1])
            sk = sink_ref[:, p * LANES:(p + 1) * LANES] * LOG2E
            mx = jnp.maximum(m_pair, sk)
            a_ = jnp.exp2(m_pair - mx)
            out = o_pair * a_ / (l_pair * a_ + jnp.exp2(sk - mx))
            o_ref[0, sb * tq:(sb + 1) * tq, p * LANES:(p + 1) * LANES] = out.astype(o_ref.dtype)


def _attention_c(proj, sink_lanes, tq=128, win=384, nsub=8):
    bsz, t, _ = proj.shape
    tab = _attn_c_tables(tq, win)
    kern = functools.partial(_attn_c_kernel, seq=t, tq=tq, win=win, nsub=nsub)
    qc0 = (3 * A_W + 3 * B_W) // C_QW
    kc0 = (3 * A_W + 3 * B_W + C_QW) // C_KW
    return pl.pallas_call(
        kern,
        grid=(bsz, t // (tq * nsub)),
        in_specs=[pl.BlockSpec((1, C_QW), lambda b, i: (0, 0)),
                  pl.BlockSpec(tab.shape, lambda b, i: (0, 0, 0, 0), pipeline_mode=pl.Buffered(1)),
                  pl.BlockSpec((1, tq * nsub, C_QW), lambda b, i: (b, i, qc0)),
                  pl.BlockSpec((1, t, C_KW), lambda b, i: (b, 0, kc0)),
                  pl.BlockSpec((1, t, C_KW), lambda b, i: (b, 0, kc0 + 1))],
        out_specs=pl.BlockSpec((1, tq * nsub, C_QW), lambda b, i: (b, i, 0)),
        out_shape=jax.ShapeDtypeStruct((bsz, t, C_QW), BF16),
        compiler_params=pltpu.CompilerParams(dimension_semantics=("parallel", "arbitrary"),
                                             vmem_limit_bytes=VMEM_LIMIT),
        name="attn_window",
    )(sink_lanes, tab, proj, proj, proj)


def _layer_norm(y, g, b):
    mu = jnp.mean(y, axis=-1, keepdims=True)
    yc = y - mu
    var = jnp.mean(yc * yc, axis=-1, keepdims=True)
    return yc * lax.rsqrt(var + LN_EPS) * g + b


def _tail_kernel(oa_ref, ob_ref, oc_ref, x_ref, mod_ref, wo_ref, wgu_ref, wd_ref, lng_ref, lnb_ref, o_ref,
                 *, alpha, hidden, ck):
    mix = (jnp.dot(oa_ref[0], wo_ref[0:A_W, :], preferred_element_type=F32)
           + jnp.dot(ob_ref[0], wo_ref[A_W:A_W + B_W, :], preferred_element_type=F32)
           + jnp.dot(oc_ref[0], wo_ref[A_W + B_W:, :], preferred_element_type=F32))
    g1 = mod_ref[0, 2:3, :]
    x1 = _layer_norm(alpha * x_ref[0] + (1.0 + g1) * mix, lng_ref[0:1, :], lnb_ref[0:1, :])
    sh2 = mod_ref[0, 3:4, :]
    sc2 = mod_ref[0, 4:5, :]
    g2 = mod_ref[0, 5:6, :]
    h2 = (x1 * (1.0 + sc2) + sh2).astype(BF16)
    acc = jnp.zeros(x1.shape, F32)
    for c0 in range(0, hidden, ck):
        gate = jnp.dot(h2, wgu_ref[:, c0:c0 + ck], preferred_element_type=F32)
        up = jnp.dot(h2, wgu_ref[:, hidden + c0:hidden + c0 + ck], preferred_element_type=F32)
        act = (gate * jax.nn.sigmoid(gate) * up).astype(BF16)
        acc = acc + jnp.dot(act, wd_ref[c0:c0 + ck, :], preferred_element_type=F32)
    o_ref[0] = _layer_norm(alpha * x1 + (1.0 + g2) * acc, lng_ref[1:2, :], lnb_ref[1:2, :])


def _tail(oa, ob, oc, x, mod6, wo_bf, wgu_bf, wd_bf, ln_g, ln_b, alpha, tm=512):
    bsz, t, d = x.shape
    hidden = wd_bf.shape[0]
    kern = functools.partial(_tail_kernel, alpha=alpha, hidden=hidden, ck=256)
    row = lambda w: pl.BlockSpec((1, tm, w), lambda b, i: (b, i, 0))
    whole = lambda shp: pl.BlockSpec(shp, lambda b, i: (0, 0), pipeline_mode=pl.Buffered(1))
    return pl.pallas_call(
        kern,
        grid=(bsz, t // tm),
        in_specs=[row(A_W), row(B_W), row(C_QW), row(d),
                  pl.BlockSpec((1, 6, d), lambda b, i: (b, 0, 0)),
                  whole(wo_bf.shape), whole(wgu_bf.shape), whole(wd_bf.shape),
                  pl.BlockSpec((2, d), lambda b, i: (0, 0)),
                  pl.BlockSpec((2, d), lambda b, i: (0, 0))],
        out_specs=row(d),
        out_shape=jax.ShapeDtypeStruct((bsz, t, d), F32),
        compiler_params=pltpu.CompilerParams(dimension_semantics=("parallel", "arbitrary"),
                                             vmem_limit_bytes=VMEM_LIMIT),
        name="outproj_ffn",
    )(oa, ob, oc, x, mod6, wo_bf, wgu_bf, wd_bf, ln_g, ln_b)


def _q_col_scale():
    cs = np.ones((1, IN_WIDTH), np.float32)
    cs[0, 0:A_W] = A_QK_DIM ** -0.5 * LOG2E
    cs[0, 3 * A_W:3 * A_W + B_W] = HEAD_DIM ** -0.5 * LOG2E
    cs[0, 3 * A_W + 3 * B_W:3 * A_W + 3 * B_W + C_QW] = HEAD_DIM ** -0.5 * LOG2E
    return jnp.asarray(cs)


def kernel(x, c, w_ada, b_ada, w_in, lam, subln_g, sink, w_out, ln_g, ln_b, w_gu, w_down):
    depth = w_ada.shape[0]
    bsz, _, d = x.shape
    alpha = (2 * depth) ** 0.25
    mod = _modulation(c, w_ada, b_ada).reshape(depth, bsz, 6, d)
    col_scale = _q_col_scale()
    for layer in range(depth):
        lambda_init = 0.8 - 0.6 * math.exp(-0.3 * layer)
        proj = _in_projection(x, mod[layer], w_in[layer].astype(BF16), col_scale)
        g2 = jnp.tile(subln_g[layer], 2).reshape(1, LANES)
        oa = _attention_a(proj, lam[layer], g2, lambda_init)
        ob = _attention_b(proj)
        oc = _attention_c(proj, jnp.repeat(sink[layer], HEAD_DIM).reshape(1, C_QW))
        x = _tail(oa, ob, oc, x, mod[layer], w_out[layer].astype(BF16), w_gu[layer].astype(BF16),
                  w_down[layer].astype(BF16), ln_g[layer], ln_b[layer], alpha)
    return x
```

```python
import functools
import math

import numpy as np
import jax
import jax.numpy as jnp
from jax import lax
from jax.experimental import pallas as pl
from jax.experimental.pallas import tpu as pltpu

F32 = jnp.float32
BF16 = jnp.bfloat16

HEAD_DIM = 64
A_HEADS = 4
A_QK_DIM = 32
B_HEADS = 6
B_DILATIONS = (16, 4, 1)
B_HALF = 64
C_HEADS = 6
C_KV_HEADS = 2
C_HALF_WINDOW = 128
N_ATTN_HEADS = 16
LN_EPS = 1e-5
NEG = -1e30
LOG2E = math.log2(math.e)

LANES = 128
A_W = A_HEADS * HEAD_DIM
B_W = B_HEADS * HEAD_DIM
C_QW = C_HEADS * HEAD_DIM
C_KW = C_KV_HEADS * HEAD_DIM
IN_WIDTH = 3 * A_W + 3 * B_W + C_QW + 2 * C_KW
VMEM_LIMIT = 56 * 1024 * 1024

_S = 2.0 ** (-8.0 * np.arange(1, N_ATTN_HEADS + 1) / N_ATTN_HEADS)
SLOPES_C = [float(v) for v in _S[:C_HEADS]]
SLOPES_A = [float(v) for v in _S[C_HEADS:C_HEADS + A_HEADS]]
SLOPES_B = [float(v) for v in _S[C_HEADS + A_HEADS:]]


def _nt_dot(a, b):
    return lax.dot_general(a, b, (((1,), (1,)), ((), ())), preferred_element_type=F32)


def _lane_iota(shape):
    return lax.broadcasted_iota(jnp.int32, shape, len(shape) - 1)


def _mod_kernel(c_ref, w_ref, b_ref, o_ref):
    c = c_ref[...]
    a = (c * jax.nn.sigmoid(c)).astype(BF16)
    o_ref[0] = jnp.dot(a, w_ref[0].astype(BF16), preferred_element_type=F32) + b_ref[0]


def _modulation(c, w_ada, b_ada):
    depth, d, n = w_ada.shape
    bsz = c.shape[0]
    tn = 1536
    return pl.pallas_call(
        _mod_kernel,
        grid=(depth, n // tn),
        in_specs=[pl.BlockSpec((bsz, d), lambda l, j: (0, 0)),
                  pl.BlockSpec((1, d, tn), lambda l, j: (l, 0, j)),
                  pl.BlockSpec((1, 1, tn), lambda l, j: (l, 0, j))],
        out_specs=pl.BlockSpec((1, bsz, tn), lambda l, j: (l, 0, j)),
        out_shape=jax.ShapeDtypeStruct((depth, bsz, n), F32),
        compiler_params=pltpu.CompilerParams(dimension_semantics=("arbitrary", "arbitrary"),
                                             vmem_limit_bytes=VMEM_LIMIT),
        name="adaln_mod",
    )(c, w_ada, b_ada.reshape(depth, 1, n))


def _proj_kernel(x_ref, mod_ref, w_ref, cs_ref, o_ref, *, tn):
    sh = mod_ref[0, 0:1, :]
    sc = mod_ref[0, 1:2, :]
    h = (x_ref[0] * (1.0 + sc) + sh).astype(BF16)
    for n0 in range(0, o_ref.shape[2], tn):
        acc = jnp.dot(h, w_ref[:, n0:n0 + tn], preferred_element_type=F32)
        o_ref[0, :, n0:n0 + tn] = (acc * cs_ref[:, n0:n0 + tn]).astype(BF16)


def _in_projection(x, mod6, w_in_bf, col_scale, tm=512):
    bsz, t, d = x.shape
    n = w_in_bf.shape[1]
    return pl.pallas_call(
        functools.partial(_proj_kernel, tn=512),
        grid=(bsz, t // tm),
        in_specs=[pl.BlockSpec((1, tm, d), lambda b, i: (b, i, 0)),
                  pl.BlockSpec((1, 6, d), lambda b, i: (b, 0, 0)),
                  pl.BlockSpec((d, n), lambda b, i: (0, 0), pipeline_mode=pl.Buffered(1)),
                  pl.BlockSpec((1, n), lambda b, i: (0, 0))],
        out_specs=pl.BlockSpec((1, tm, n), lambda b, i: (b, i, 0)),
        out_shape=jax.ShapeDtypeStruct((bsz, t, n), BF16),
        compiler_params=pltpu.CompilerParams(dimension_semantics=("parallel", "arbitrary"),
                                             vmem_limit_bytes=VMEM_LIMIT),
        name="in_proj",
    )(x, mod6, w_in_bf, col_scale)


def _attn_a_tables(tq, tk):
    rel = (np.arange(tq)[:, None] - np.arange(tk)[None, :]).astype(np.float64)
    tabs = [np.stack([-s * LOG2E * np.abs(rel + c * tq) for c in range(tk // tq)]) for s in SLOPES_A]
    return jnp.asarray(np.stack(tabs).astype(np.float32))


def _alibi_lane(head):
    return 3 * A_QK_DIM if head % 2 == 0 else 0


def _attn_a_alibi_lanes(seq):
    pos = np.arange(seq, dtype=np.float64)
    bf = jnp.bfloat16
    kside = np.zeros((A_HEADS, seq, LANES), np.float32)
    qside = np.zeros((A_HEADS, seq, LANES), np.float32)
    for h, s in enumerate(SLOPES_A):
        a0 = _alibi_lane(h)
        x = (pos * (s * LOG2E)).astype(np.float32)
        parts, rest = [], x
        for _ in range(3):
            part = np.asarray(rest.astype(bf), np.float32)
            parts.append(part)
            rest = (rest - part).astype(np.float32)
        for i, part in enumerate(parts):
            kside[h, :, a0 + i] = part
            qside[h, :, a0 + 3 + i] = -part
        kside[h, :, a0 + 3:a0 + 6] = 1.0
        qside[h, :, a0:a0 + 3] = 1.0
    return jnp.asarray(kside, BF16), jnp.asarray(qside, BF16)


def _attn_a_kernel(lam_ref, g_ref, tab_ref, kside_ref, qside_ref, q_ref, k_ref, v_ref, o_ref,
                   kaug, vaug, qv_sc, t_even, t_odd, acc_sc, *, tq, tk, seq, lambda_init):
    qi = pl.program_id(1)
    r = tk // tq
    nkv = seq // tk
    kd = qi // r
    var_d = qi % r
    i0 = pl.multiple_of(qi * tq, tq)
    lane = _lane_iota((1, LANES))
    first = lane < HEAD_DIM
    nmap = 2 * A_HEADS
    nlt = tk // LANES
    rows_per_build = 512

    def group_lanes(gm):
        lo = A_QK_DIM * (gm % 4)
        return (lane >= lo) & (lane < lo + A_QK_DIM)

    @pl.when(qi == 0)
    def _build_kv_copies():
        for rc in range(seq // rows_per_build):
            rows = slice(rc * rows_per_build, (rc + 1) * rows_per_build)
            for p in range(A_HEADS // 2):
                kp = k_ref[0, rows, p * LANES:(p + 1) * LANES]
                vp = v_ref[0, rows, p * LANES:(p + 1) * LANES]
                for a in range(2):
                    vaug[2 * p + a, rows, :] = jnp.where(first if a == 0 else jnp.logical_not(first), vp,
                                                         jnp.ones_like(vp))
                for g in range(4):
                    gm = 4 * p + g
                    kaug[gm, rows, :] = jnp.where(group_lanes(gm), kp, kside_ref[gm // 2, rows, :])

    for p in range(A_HEADS // 2):
        qp = q_ref[0, :, p * LANES:(p + 1) * LANES]
        for g in range(4):
            gm = 4 * p + g
            side = qside_ref[gm // 2, pl.ds(i0, tq), :]
            qv_sc[gm, 0] = jnp.where(group_lanes(gm), qp, jnp.zeros_like(qp))
            qv_sc[gm, 1] = jnp.where(group_lanes(gm), qp, side)
            qv_sc[gm, 2] = jnp.where(group_lanes(gm), qp, -side)

    chunk_ids = [kd] + [(kd + cc) & (nkv - 1) for cc in range(1, nkv)]
    starts = [pl.multiple_of(cid * tk, tk) for cid in chunk_ids]
    q_variant = [0] + [jnp.where(cid < kd, 1, 2) for cid in chunk_ids[1:]]

    def logits_chunk(gm, cc, mx):
        t_sc = t_odd if gm % 2 else t_even
        s = _nt_dot(qv_sc[gm, q_variant[cc]], kaug[gm, pl.ds(starts[cc], tk), :])
        if cc == 0:
            s = s + tab_ref[gm // 2, var_d]
        t_sc[:, cc * tk:(cc + 1) * tk] = s
        cm = s[:, 0:LANES]
        for lt in range(1, nlt):
            cm = jnp.maximum(cm, s[:, lt * LANES:(lt + 1) * LANES])
        return cm if mx is None else jnp.maximum(mx, cm)

    def row_max(mx):
        return jnp.broadcast_to(jnp.max(mx, axis=1, keepdims=True), (tq, LANES))

    def pv_chunk(gm, cc, mb, acc):
        t_sc = t_odd if gm % 2 else t_even
        blocks = []
        for rb in range(0, tq, 64):
            blocks.append(jnp.concatenate(
                [jnp.exp2(t_sc[rb:rb + 64, cc * tk + lt * LANES:cc * tk + (lt + 1) * LANES]
                          - mb[rb:rb + 64, :]).astype(BF16) for lt in range(nlt)], axis=1))
        pv = jnp.dot(jnp.concatenate(blocks, axis=0), vaug[gm // 2, pl.ds(starts[cc], tk), :],
                     preferred_element_type=F32)
        return pv if acc is None else acc + pv

    mx = None
    for cc in range(nkv):
        mx = logits_chunk(0, cc, mx)
    mb = row_max(mx)
    for gm in range(nmap):
        mx, acc = None, None
        for cc in range(nkv):
            if gm + 1 < nmap:
                mx = logits_chunk(gm + 1, cc, mx)
            acc = pv_chunk(gm, cc, mb, acc)
        acc_sc[gm] = acc
        if gm + 1 < nmap:
            mb = row_max(mx)

    lf = lam_ref[...]
    lam_full = (jnp.exp(jnp.sum(lf[0:1] * lf[1:2], axis=(0, 1), keepdims=True))
                - jnp.exp(jnp.sum(lf[2:3] * lf[3:4], axis=(0, 1), keepdims=True)) + lambda_init)
    for p in range(A_HEADS // 2):
        d_heads = []
        for a in range(2):
            o_m = []
            for m2 in range(2):
                acc = acc_sc[4 * p + 2 * a + m2]
                o_m.append(acc / pltpu.roll(acc, HEAD_DIM, 1))
            d_heads.append(o_m[0] - lam_full * o_m[1])
        dp = jnp.where(first, d_heads[0], d_heads[1])
        sq = dp * dp
        ms0 = jnp.sum(jnp.where(first, sq, 0.0), axis=1, keepdims=True) * (1.0 / HEAD_DIM)
        ms1 = jnp.sum(jnp.where(first, 0.0, sq), axis=1, keepdims=True) * (1.0 / HEAD_DIM)
        ms = jnp.where(first, ms0, ms1)
        out = dp * lax.rsqrt(ms + LN_EPS) * g_ref[...] * (1.0 - lambda_init)
        o_ref[0, :, p * LANES:(p + 1) * LANES] = out.astype(o_ref.dtype)


def _attention_a(proj, lam_l, g2, lambda_init, tq=256, tk=512):
    bsz, t, _ = proj.shape
    tab = _attn_a_tables(tq, tk)
    kside, qside = _attn_a_alibi_lanes(t)
    nmap = 2 * A_HEADS
    kern = functools.partial(_attn_a_kernel, tq=tq, tk=tk, seq=t, lambda_init=lambda_init)
    return pl.pallas_call(
        kern,
        grid=(bsz, t // tq),
        in_specs=[pl.BlockSpec((4, A_QK_DIM), lambda b, i: (0, 0)),
                  pl.BlockSpec((1, LANES), lambda b, i: (0, 0)),
                  pl.BlockSpec(tab.shape, lambda b, i: (0, 0, 0, 0), pipeline_mode=pl.Buffered(1)),
                  pl.BlockSpec(kside.shape, lambda b, i: (0, 0, 0), pipeline_mode=pl.Buffered(1)),
                  pl.BlockSpec(qside.shape, lambda b, i: (0, 0, 0), pipeline_mode=pl.Buffered(1)),
                  pl.BlockSpec((1, tq, A_W), lambda b, i: (b, i, 0)),
                  pl.BlockSpec((1, t, A_W), lambda b, i: (b, 0, 1)),
                  pl.BlockSpec((1, t, A_W), lambda b, i: (b, 0, 2))],
        out_specs=pl.BlockSpec((1, tq, A_W), lambda b, i: (b, i, 0)),
        out_shape=jax.ShapeDtypeStruct((bsz, t, A_W), BF16),
        scratch_shapes=[pltpu.VMEM((nmap, t, LANES), BF16),
                        pltpu.VMEM((A_HEADS, t, LANES), BF16),
                        pltpu.VMEM((nmap, 3, tq, LANES), BF16),
                        pltpu.VMEM((tq, t), F32),
                        pltpu.VMEM((tq, t), F32),
                        pltpu.VMEM((nmap, tq, LANES), F32)],
        compiler_params=pltpu.CompilerParams(dimension_semantics=("parallel", "arbitrary"),
                                             vmem_limit_bytes=VMEM_LIMIT),
        name="attn_diff",
    )(lam_l, g2, tab, kside, qside, proj, proj, proj)


def _attn_b_kernel(sl_ref, q_ref, k_ref, v_ref, o_ref, qf, kf, vf, q4, k4, v4, m_st, l_st, o_st, bias_sc,
                   *, seq, tq, win, unroll):
    lane = _lane_iota((1, LANES))
    first = lane < HEAD_DIM
    halves = (first, jnp.logical_not(first))
    chunk = 512

    def prep(c, carry):
        r0 = pl.multiple_of(c * chunk, chunk)
        rows = pl.ds(r0, chunk)
        qf[rows, :] = q_ref[0, rows, :].astype(F32)
        kf[rows, :] = k_ref[0, rows, :].astype(F32)
        vf[rows, :] = v_ref[0, rows, :].astype(F32)
        return carry

    lax.fori_loop(0, seq // chunk, prep, 0)

    len4 = seq // 4
    for src_ref, dst_ref in ((qf, q4), (kf, k4), (vf, v4)):
        for r4 in range(4):
            for c0 in range(0, len4, chunk):
                dst_ref[r4 * len4 + c0:r4 * len4 + c0 + chunk, :] = src_ref[pl.ds(r4 + 4 * c0, chunk, stride=4), :]

    relb = (lax.broadcasted_iota(jnp.int32, (tq, win), 0) - lax.broadcasted_iota(jnp.int32, (tq, win), 1))

    for d in B_DILATIONS:
        length = seq // d
        nblk = length // tq

        for case in range(3):
            adist = jnp.abs(relb + case * B_HALF)
            adf = adist.astype(F32)
            for a in range(2):
                sl = sl_ref[0, a:a + 1, 0:1] * float(d)
                bias_sc[case, a] = jnp.where(adist <= B_HALF, -sl * adf, NEG)

        def state_rows(res, j, size, d=d):
            return pl.ds(j, size) if d == 1 else pl.ds(res + j * d, size, stride=d)

        def data_rows(res, j, size, d=d):
            if d == 1:
                return pl.ds(j, size)
            if d == 4:
                return pl.ds(res * len4 + j, size)
            return pl.ds((res % 4) * len4 + res // 4 + j * (d // 4), size, stride=d // 4)

        q_src, k_src, v_src = (qf, kf, vf) if d == 1 else (q4, k4, v4)

        merge = d != B_DILATIONS[0]

        def body(it, carry, length=length, nblk=nblk, state_rows=state_rows, data_rows=data_rows,
                 q_src=q_src, k_src=k_src, v_src=v_src, merge=merge):
            rqs, cases, qs, ks, vs = [], [], [], [], []
            for u in range(unroll):
                idx = it * unroll + u
                res = idx // nblk
                j0 = (idx % nblk) * tq
                start = jnp.clip(j0 - B_HALF, 0, length - win)
                rqs.append(state_rows(res, j0, tq))
                cases.append((j0 - start) // B_HALF)
                qs.append(q_src[data_rows(res, j0, tq), :])
                ks.append(k_src[data_rows(res, start, win), :].astype(BF16))
                vs.append(v_src[data_rows(res, start, win), :])
            items = [(u, a) for u in range(unroll) for a in range(2)]
            ss = [_nt_dot(jnp.where(halves[a], qs[u], 0.0).astype(BF16), ks[u]) for u, a in items]
            ts = [ss[i] + bias_sc[cases[u], a] for i, (u, a) in enumerate(items)]
            mbs = [jnp.max(t, axis=1, keepdims=True) for t in ts]
            prs = [jnp.exp2(t - mb).astype(BF16) for t, mb in zip(ts, mbs)]
            pvs = [jnp.dot(prs[i], jnp.where(halves[a], vs[u], 1.0).astype(BF16), preferred_element_type=F32)
                   for i, (u, a) in enumerate(items)]
            if merge:
                m_prev = [m_st[rq, :] for rq in rqs]
                l_prev = [l_st[rq, :] for rq in rqs]
                o_prev = [o_st[rq, :] for rq in rqs]
            for u in range(unroll):
                pv0, pv1 = pvs[2 * u], pvs[2 * u + 1]
                o_blk = jnp.where(first, pv0, pv1)
                l_blk = pltpu.roll(jnp.where(first, pv1, pv0), HEAD_DIM, 1)
                m_blk = jnp.where(first, mbs[2 * u], mbs[2 * u + 1])
                if not merge:
                    l_st[rqs[u], :] = l_blk
                    o_st[rqs[u], :] = o_blk
                    m_st[rqs[u], :] = jnp.broadcast_to(m_blk, (tq, LANES))
                    continue
                m_new = jnp.maximum(m_prev[u], m_blk)
                a_prev = jnp.exp2(m_prev[u] - m_new)
                a_blk = jnp.exp2(m_blk - m_new)
                l_st[rqs[u], :] = a_prev * l_prev[u] + a_blk * l_blk
                o_st[rqs[u], :] = a_prev * o_prev[u] + a_blk * o_blk
                m_st[rqs[u], :] = m_new
            return carry

        lax.fori_loop(0, seq // tq // unroll, body, 0)

    def fin(c, carry):
        r0 = pl.multiple_of(c * chunk, chunk)
        rows = pl.ds(r0, chunk)
        o_ref[0, rows, :] = (o_st[rows, :] / l_st[rows, :]).astype(o_ref.dtype)
        return carry

    lax.fori_loop(0, seq // chunk, fin, 0)


def _attention_b(proj, tq=128, win=256, unroll=32):
    bsz, t, _ = proj.shape
    npair = B_HEADS // 2
    sl = np.zeros((npair, 8, LANES), np.float32)
    for h, s in enumerate(SLOPES_B):
        sl[h // 2, h % 2, :] = s * LOG2E
    qb0 = 3 * A_W // LANES
    kern = functools.partial(_attn_b_kernel, seq=t, tq=tq, win=win, unroll=unroll)
    seq_spec = lambda off: pl.BlockSpec((1, t, LANES), lambda b, p, off=off: (b, 0, off + p))
    return pl.pallas_call(
        kern,
        grid=(bsz, npair),
        in_specs=[pl.BlockSpec((1, 8, LANES), lambda b, p: (p, 0, 0)),
                  seq_spec(qb0), seq_spec(qb0 + B_W // LANES), seq_spec(qb0 + 2 * B_W // LANES)],
        out_specs=pl.BlockSpec((1, t, LANES), lambda b, p: (b, 0, p)),
        out_shape=jax.ShapeDtypeStruct((bsz, t, B_W), BF16),
        scratch_shapes=[pltpu.VMEM((t, LANES), F32) for _ in range(9)] + [pltpu.VMEM((3, 2, tq, win), F32)],
        compiler_params=pltpu.CompilerParams(dimension_semantics=("parallel", "arbitrary"),
                                             vmem_limit_bytes=VMEM_LIMIT),
        name="attn_dilated",
    )(jnp.asarray(sl), proj, proj, proj)


def _attn_c_tables(tq, win):
    rel = np.arange(tq)[:, None] - np.arange(win)[None, :]
    tabs = []
    for case in range(3):
        dist = np.abs(rel + case * C_HALF_WINDOW).astype(np.float64)
        tabs.append(np.stack([np.where(dist <= C_HALF_WINDOW, -s * LOG2E * dist, NEG) for s in SLOPES_C]))
    return jnp.asarray(np.stack(tabs).astype(np.float32))


def _attn_c_kernel(sink_ref, tab_ref, q_ref, k_ref, v_ref, o_ref, *, seq, tq, win, nsub):
    qi = pl.program_id(1)
    lane = _lane_iota((1, LANES))
    first = lane < HEAD_DIM
    halves = (first, jnp.logical_not(first))
    rep = C_HEADS // C_KV_HEADS
    kv_of = [h // rep for h in range(C_HEADS)]
    items = [(sb, h) for sb in range(nsub) for h in range(C_HEADS)]
    cases, k_same, k_swap, v_aug = [], [], [], []
    for sb in range(nsub):
        i0 = (qi * nsub + sb) * tq
        start = pl.multiple_of(jnp.clip(i0 - C_HALF_WINDOW, 0, seq - win), C_HALF_WINDOW)
        cases.append((i0 - start) // C_HALF_WINDOW)
        k2 = k_ref[0, pl.ds(start, win), :].astype(F32)
        v2 = v_ref[0, pl.ds(start, win), :]
        k_same.append(k2.astype(BF16))
        k_swap.append(pltpu.roll(k2, HEAD_DIM, 1).astype(BF16))
        v_aug.append([jnp.where(halves[g], v2, jnp.ones_like(v2)) for g in range(C_KV_HEADS)])

    def q_masked(sb, h):
        qp = q_ref[0, sb * tq:(sb + 1) * tq, (h // 2) * LANES:(h // 2 + 1) * LANES]
        return jnp.where(halves[h % 2], qp, jnp.zeros_like(qp))

    ss = [_nt_dot(q_masked(sb, h), k_same[sb] if h % 2 == kv_of[h] else k_swap[sb]) for sb, h in items]
    ts = [ss[i] + tab_ref[cases[sb], h] for i, (sb, h) in enumerate(items)]
    ms = [jnp.max(t, axis=1, keepdims=True) for t in ts]
    prs = [jnp.exp2(t - m).astype(BF16) for t, m in zip(ts, ms)]
    pvs = [jnp.dot(prs[i], v_aug[sb][kv_of[h]], preferred_element_type=F32) for i, (sb, h) in enumerate(items)]
    swp = [pltpu.roll(pv, HEAD_DIM, 1) for pv in pvs]
    outs = [pvs[i] if h % 2 == kv_of[h] else swp[i] for i, (sb, h) in enumerate(items)]
    dens = [swp[i] if h % 2 == kv_of[h] else pvs[i] for i, (sb, h) in enumerate(items)]
    for sb in range(nsub):
        for p in range(C_HEADS // 2):
            i0, i1 = sb * C_HEADS + 2 * p, sb * C_HEADS + 2 * p + 1
            m_pair = jnp.where(first, ms[i0], ms[i1])
            l_pair = jnp.where(first, dens[i0], dens[i1])
            o_pair = jnp.where(first, outs[i0], outs[i1])
            sk = sink_ref[:, p * LANES:(p + 1) * LANES] * LOG2E
            mx = jnp.maximum(m_pair, sk)
            a_ = jnp.exp2(m_pair - mx)
            out = o_pair * a_ / (l_pair * a_ + jnp.exp2(sk - mx))
            o_ref[0, sb * tq:(sb + 1) * tq, p * LANES:(p + 1) * LANES] = out.astype(o_ref.dtype)


def _attention_c(proj, sink_lanes, tq=128, win=384, nsub=8):
    bsz, t, _ = proj.shape
    tab = _attn_c_tables(tq, win)
    kern = functools.partial(_attn_c_kernel, seq=t, tq=tq, win=win, nsub=nsub)
    qc0 = (3 * A_W + 3 * B_W) // C_QW
    kc0 = (3 * A_W + 3 * B_W + C_QW) // C_KW
    return pl.pallas_call(
        kern,
        grid=(bsz, t // (tq * nsub)),
        in_specs=[pl.BlockSpec((1, C_QW), lambda b, i: (0, 0)),
                  pl.BlockSpec(tab.shape, lambda b, i: (0, 0, 0, 0), pipeline_mode=pl.Buffered(1)),
                  pl.BlockSpec((1, tq * nsub, C_QW), lambda b, i: (b, i, qc0)),
                  pl.BlockSpec((1, t, C_KW), lambda b, i: (b, 0, kc0)),
                  pl.BlockSpec((1, t, C_KW), lambda b, i: (b, 0, kc0 + 1))],
        out_specs=pl.BlockSpec((1, tq * nsub, C_QW), lambda b, i: (b, i, 0)),
        out_shape=jax.ShapeDtypeStruct((bsz, t, C_QW), BF16),
        compiler_params=pltpu.CompilerParams(dimension_semantics=("parallel", "arbitrary"),
                                             vmem_limit_bytes=VMEM_LIMIT),
        name="attn_window",
    )(sink_lanes, tab, proj, proj, proj)


def _layer_norm(y, g, b):
    mu = jnp.mean(y, axis=-1, keepdims=True)
    yc = y - mu
    var = jnp.mean(yc * yc, axis=-1, keepdims=True)
    return yc * lax.rsqrt(var + LN_EPS) * g + b


def _tail_kernel(oa_ref, ob_ref, oc_ref, x_ref, mod_ref, wo_ref, wgu_ref, wd_ref, lng_ref, lnb_ref, o_ref,
                 *, alpha, hidden, ck):
    mix = (jnp.dot(oa_ref[0], wo_ref[0:A_W, :], preferred_element_type=F32)
           + jnp.dot(ob_ref[0], wo_ref[A_W:A_W + B_W, :], preferred_element_type=F32)
           + jnp.dot(oc_ref[0], wo_ref[A_W + B_W:, :], preferred_element_type=F32))
    g1 = mod_ref[0, 2:3, :]
    x1 = _layer_norm(alpha * x_ref[0] + (1.0 + g1) * mix, lng_ref[0:1, :], lnb_ref[0:1, :])
    sh2 = mod_ref[0, 3:4, :]
    sc2 = mod_ref[0, 4:5, :]
    g2 = mod_ref[0, 5:6, :]
    h2 = (x1 * (1.0 + sc2) + sh2).astype(BF16)
    acc = jnp.zeros(x1.shape, F32)
    for c0 in range(0, hidden, ck):
        gate = jnp.dot(h2, wgu_ref[:, c0:c0 + ck], preferred_element_type=F32)
        up = jnp.dot(h2, wgu_ref[:, hidden + c0:hidden + c0 + ck], preferred_element_type=F32)
        act = (gate * jax.nn.sigmoid(gate) * up).astype(BF16)
        acc = acc + jnp.dot(act, wd_ref[c0:c0 + ck, :], preferred_element_type=F32)
    o_ref[0] = _layer_norm(alpha * x1 + (1.0 + g2) * acc, lng_ref[1:2, :], lnb_ref[1:2, :])


def _tail(oa, ob, oc, x, mod6, wo_bf, wgu_bf, wd_bf, ln_g, ln_b, alpha, tm=512):
    bsz, t, d = x.shape
    hidden = wd_bf.shape[0]
    kern = functools.partial(_tail_kernel, alpha=alpha, hidden=hidden, ck=256)
    row = lambda w: pl.BlockSpec((1, tm, w), lambda b, i: (b, i, 0))
    whole = lambda shp: pl.BlockSpec(shp, lambda b, i: (0, 0), pipeline_mode=pl.Buffered(1))
    return pl.pallas_call(
        kern,
        grid=(bsz, t // tm),
        in_specs=[row(A_W), row(B_W), row(C_QW), row(d),
                  pl.BlockSpec((1, 6, d), lambda b, i: (b, 0, 0)),
                  whole(wo_bf.shape), whole(wgu_bf.shape), whole(wd_bf.shape),
                  pl.BlockSpec((2, d), lambda b, i: (0, 0)),
                  pl.BlockSpec((2, d), lambda b, i: (0, 0))],
        out_specs=row(d),
        out_shape=jax.ShapeDtypeStruct((bsz, t, d), F32),
        compiler_params=pltpu.CompilerParams(dimension_semantics=("parallel", "arbitrary"),
                                             vmem_limit_bytes=VMEM_LIMIT),
        name="outproj_ffn",
    )(oa, ob, oc, x, mod6, wo_bf, wgu_bf, wd_bf, ln_g, ln_b)


def _q_col_scale():
    cs = np.ones((1, IN_WIDTH), np.float32)
    cs[0, 0:A_W] = A_QK_DIM ** -0.5 * LOG2E
    cs[0, 3 * A_W:3 * A_W + B_W] = HEAD_DIM ** -0.5 * LOG2E
    cs[0, 3 * A_W + 3 * B_W:3 * A_W + 3 * B_W + C_QW] = HEAD_DIM ** -0.5 * LOG2E
    return jnp.asarray(cs)


def kernel(x, c, w_ada, b_ada, w_in, lam, subln_g, sink, w_out, ln_g, ln_b, w_gu, w_down):
    depth = w_ada.shape[0]
    bsz, _, d = x.shape
    alpha = (2 * depth) ** 0.25
    mod = _modulation(c, w_ada, b_ada).reshape(depth, bsz, 6, d)
    col_scale = _q_col_scale()
    for layer in range(depth):
        lambda_init = 0.8 - 0.6 * math.exp(-0.3 * layer)
        proj = _in_projection(x, mod[layer], w_in[layer].astype(BF16), col_scale)
        g2 = jnp.tile(subln_g[layer], 2).reshape(1, LANES)
        oa = _attention_a(proj, lam[layer], g2, lambda_init)
        ob = _attention_b(proj)
        oc = _attention_c(proj, jnp.repeat(sink[layer], HEAD_DIM).reshape(1, C_QW))
        x = _tail(oa, ob, oc, x, mod[layer], w_out[layer].astype(BF16), w_gu[layer].astype(BF16),
                  w_down[layer].astype(BF16), ln_g[layer], ln_b[layer], alpha)
    return x
```

```python
import functools
import math

import numpy as np
import jax
import jax.numpy as jnp
from jax import lax
from jax.experimental import pallas as pl
from jax.experimental.pallas import tpu as pltpu

F32 = jnp.float32
BF16 = jnp.bfloat16

HEAD_DIM = 64
A_HEADS = 4
A_QK_DIM = 32
B_HEADS = 6
B_DILATIONS = (16, 4, 1)
B_HALF = 64
C_HEADS = 6
C_KV_HEADS = 2
C_HALF_WINDOW = 128
N_ATTN_HEADS = 16
LN_EPS = 1e-5
NEG = -1e30
LOG2E = math.log2(math.e)

LANES = 128
A_W = A_HEADS * HEAD_DIM
B_W = B_HEADS * HEAD_DIM
C_QW = C_HEADS * HEAD_DIM
C_KW = C_KV_HEADS * HEAD_DIM
IN_WIDTH = 3 * A_W + 3 * B_W + C_QW + 2 * C_KW
VMEM_LIMIT = 56 * 1024 * 1024

_S = 2.0 ** (-8.0 * np.arange(1, N_ATTN_HEADS + 1) / N_ATTN_HEADS)
SLOPES_C = [float(v) for v in _S[:C_HEADS]]
SLOPES_A = [float(v) for v in _S[C_HEADS:C_HEADS + A_HEADS]]
SLOPES_B = [float(v) for v in _S[C_HEADS + A_HEADS:]]


def _nt_dot(a, b):
    return lax.dot_general(a, b, (((1,), (1,)), ((), ())), preferred_element_type=F32)


def _lane_iota(shape):
    return lax.broadcasted_iota(jnp.int32, shape, len(shape) - 1)


def _mod_kernel(c_ref, w_ref, b_ref, o_ref):
    c = c_ref[...]
    a = (c * jax.nn.sigmoid(c)).astype(BF16)
    o_ref[0] = jnp.dot(a, w_ref[0].astype(BF16), preferred_element_type=F32) + b_ref[0]


def _modulation(c, w_ada, b_ada):
    depth, d, n = w_ada.shape
    bsz = c.shape[0]
    tn = 1536
    return pl.pallas_call(
        _mod_kernel,
        grid=(depth, n // tn),
        in_specs=[pl.BlockSpec((bsz, d), lambda l, j: (0, 0)),
                  pl.BlockSpec((1, d, tn), lambda l, j: (l, 0, j)),
                  pl.BlockSpec((1, 1, tn), lambda l, j: (l, 0, j))],
        out_specs=pl.BlockSpec((1, bsz, tn), lambda l, j: (l, 0, j)),
        out_shape=jax.ShapeDtypeStruct((depth, bsz, n), F32),
        compiler_params=pltpu.CompilerParams(dimension_semantics=("arbitrary", "arbitrary"),
                                             vmem_limit_bytes=VMEM_LIMIT),
        name="adaln_mod",
    )(c, w_ada, b_ada.reshape(depth, 1, n))


def _proj_kernel(x_ref, mod_ref, w_ref, cs_ref, o_ref, *, tn):
    sh = mod_ref[0, 0:1, :]
    sc = mod_ref[0, 1:2, :]
    h = (x_ref[0] * (1.0 + sc) + sh).astype(BF16)
    for n0 in range(0, o_ref.shape[2], tn):
        acc = jnp.dot(h, w_ref[:, n0:n0 + tn], preferred_element_type=F32)
        o_ref[0, :, n0:n0 + tn] = (acc * cs_ref[:, n0:n0 + tn]).astype(BF16)


def _in_projection(x, mod6, w_in_bf, layer, col_scale, tm=512):
    bsz, t, d = x.shape
    n = w_in_bf.shape[2]
    return pl.pallas_call(
        functools.partial(_proj_kernel, tn=512),
        grid=(bsz, t // tm),
        in_specs=[pl.BlockSpec((1, tm, d), lambda b, i: (b, i, 0)),
                  pl.BlockSpec((1, 6, d), lambda b, i: (b, 0, 0)),
                  pl.BlockSpec((None, d, n), lambda b, i: (layer, 0, 0), pipeline_mode=pl.Buffered(1)),
                  pl.BlockSpec((1, n), lambda b, i: (0, 0))],
        out_specs=pl.BlockSpec((1, tm, n), lambda b, i: (b, i, 0)),
        out_shape=jax.ShapeDtypeStruct((bsz, t, n), BF16),
        compiler_params=pltpu.CompilerParams(dimension_semantics=("parallel", "arbitrary"),
                                             vmem_limit_bytes=VMEM_LIMIT),
        name="in_proj",
    )(x, mod6, w_in_bf, col_scale)


def _attn_a_tables(tq, tk):
    rel = (np.arange(tq)[:, None] - np.arange(tk)[None, :]).astype(np.float64)
    tabs = [np.stack([-s * LOG2E * np.abs(rel + c * tq) for c in range(tk // tq)]) for s in SLOPES_A]
    return jnp.asarray(np.stack(tabs).astype(np.float32))


def _alibi_lane(head):
    return 3 * A_QK_DIM if head % 2 == 0 else 0


def _attn_a_alibi_lanes(seq):
    pos = np.arange(seq, dtype=np.float64)
    bf = jnp.bfloat16
    kside = np.zeros((A_HEADS, seq, LANES), np.float32)
    qside = np.zeros((A_HEADS, seq, LANES), np.float32)
    for h, s in enumerate(SLOPES_A):
        a0 = _alibi_lane(h)
        x = (pos * (s * LOG2E)).astype(np.float32)
        parts, rest = [], x
        for _ in range(3):
            part = np.asarray(rest.astype(bf), np.float32)
            parts.append(part)
            rest = (rest - part).astype(np.float32)
        for i, part in enumerate(parts):
            kside[h, :, a0 + i] = part
            qside[h, :, a0 + 3 + i] = -part
        kside[h, :, a0 + 3:a0 + 6] = 1.0
        qside[h, :, a0:a0 + 3] = 1.0
    return jnp.asarray(kside, BF16), jnp.asarray(qside, BF16)


def _attn_a_kernel(lam_ref, g_ref, tab_ref, kside_ref, qside_ref, q_ref, k_ref, v_ref, o_ref,
                   kaug, vaug, qv_sc, t_even, t_odd, acc_sc, *, tq, tk, seq, lambda_init):
    qi = pl.program_id(1)
    r = tk // tq
    nkv = seq // tk
    kd = qi // r
    var_d = qi % r
    i0 = pl.multiple_of(qi * tq, tq)
    lane = _lane_iota((1, LANES))
    first = lane < HEAD_DIM
    nmap = 2 * A_HEADS
    nlt = tk // LANES
    rows_per_build = 512

    def group_lanes(gm):
        lo = A_QK_DIM * (gm % 4)
        return (lane >= lo) & (lane < lo + A_QK_DIM)

    @pl.when(qi == 0)
    def _build_kv_copies():
        for rc in range(seq // rows_per_build):
            rows = slice(rc * rows_per_build, (rc + 1) * rows_per_build)
            for p in range(A_HEADS // 2):
                kp = k_ref[0, rows, p * LANES:(p + 1) * LANES]
                vp = v_ref[0, rows, p * LANES:(p + 1) * LANES]
                for a in range(2):
                    vaug[2 * p + a, rows, :] = jnp.where(first if a == 0 else jnp.logical_not(first), vp,
                                                         jnp.ones_like(vp))
                for g in range(4):
                    gm = 4 * p + g
                    kaug[gm, rows, :] = jnp.where(group_lanes(gm), kp, kside_ref[gm // 2, rows, :])

    for p in range(A_HEADS // 2):
        qp = q_ref[0, :, p * LANES:(p + 1) * LANES]
        for g in range(4):
            gm = 4 * p + g
            side = qside_ref[gm // 2, pl.ds(i0, tq), :]
            qv_sc[gm, 0] = jnp.where(group_lanes(gm), qp, jnp.zeros_like(qp))
            qv_sc[gm, 1] = jnp.where(group_lanes(gm), qp, side)
            qv_sc[gm, 2] = jnp.where(group_lanes(gm), qp, -side)

    chunk_ids = [kd] + [(kd + cc) & (nkv - 1) for cc in range(1, nkv)]
    starts = [pl.multiple_of(cid * tk, tk) for cid in chunk_ids]
    q_variant = [0] + [jnp.where(cid < kd, 1, 2) for cid in chunk_ids[1:]]

    def logits_chunk(gm, cc, mx):
        t_sc = t_odd if gm % 2 else t_even
        s = _nt_dot(qv_sc[gm, q_variant[cc]], kaug[gm, pl.ds(starts[cc], tk), :])
        if cc == 0:
            s = s + tab_ref[gm // 2, var_d]
        t_sc[:, cc * tk:(cc + 1) * tk] = s
        cm = s[:, 0:LANES]
        for lt in range(1, nlt):
            cm = jnp.maximum(cm, s[:, lt * LANES:(lt + 1) * LANES])
        return cm if mx is None else jnp.maximum(mx, cm)

    def row_max(mx):
        return jnp.broadcast_to(jnp.max(mx, axis=1, keepdims=True), (tq, LANES))

    def pv_chunk(gm, cc, mb, acc):
        t_sc = t_odd if gm % 2 else t_even
        blocks = []
        for rb in range(0, tq, 64):
            blocks.append(jnp.concatenate(
                [jnp.exp2(t_sc[rb:rb + 64, cc * tk + lt * LANES:cc * tk + (lt + 1) * LANES]
                          - mb[rb:rb + 64, :]).astype(BF16) for lt in range(nlt)], axis=1))
        pv = jnp.dot(jnp.concatenate(blocks, axis=0), vaug[gm // 2, pl.ds(starts[cc], tk), :],
                     preferred_element_type=F32)
        return pv if acc is None else acc + pv

    mx = None
    for cc in range(nkv):
        mx = logits_chunk(0, cc, mx)
    mb = row_max(mx)
    for gm in range(nmap):
        mx, acc = None, None
        for cc in range(nkv):
            if gm + 1 < nmap:
                mx = logits_chunk(gm + 1, cc, mx)
            acc = pv_chunk(gm, cc, mb, acc)
        acc_sc[gm] = acc
        if gm + 1 < nmap:
            mb = row_max(mx)

    lf = lam_ref[...]
    lam_full = (jnp.exp(jnp.sum(lf[0:1] * lf[1:2], axis=(0, 1), keepdims=True))
                - jnp.exp(jnp.sum(lf[2:3] * lf[3:4], axis=(0, 1), keepdims=True)) + lambda_init)
    for p in range(A_HEADS // 2):
        d_heads = []
        for a in range(2):
            o_m = []
            for m2 in range(2):
                acc = acc_sc[4 * p + 2 * a + m2]
                o_m.append(acc / pltpu.roll(acc, HEAD_DIM, 1))
            d_heads.append(o_m[0] - lam_full * o_m[1])
        dp = jnp.where(first, d_heads[0], d_heads[1])
        sq = dp * dp
        ms0 = jnp.sum(jnp.where(first, sq, 0.0), axis=1, keepdims=True) * (1.0 / HEAD_DIM)
        ms1 = jnp.sum(jnp.where(first, 0.0, sq), axis=1, keepdims=True) * (1.0 / HEAD_DIM)
        ms = jnp.where(first, ms0, ms1)
        out = dp * lax.rsqrt(ms + LN_EPS) * g_ref[...] * (1.0 - lambda_init)
        o_ref[0, :, p * LANES:(p + 1) * LANES] = out.astype(o_ref.dtype)


def _attention_a(proj, lam_l, g2, lambda_init, tq=256, tk=512):
    bsz, t, _ = proj.shape
    tab = _attn_a_tables(tq, tk)
    kside, qside = _attn_a_alibi_lanes(t)
    nmap = 2 * A_HEADS
    kern = functools.partial(_attn_a_kernel, tq=tq, tk=tk, seq=t, lambda_init=lambda_init)
    return pl.pallas_call(
        kern,
        grid=(bsz, t // tq),
        in_specs=[pl.BlockSpec((4, A_QK_DIM), lambda b, i: (0, 0)),
                  pl.BlockSpec((1, LANES), lambda b, i: (0, 0)),
                  pl.BlockSpec(tab.shape, lambda b, i: (0, 0, 0, 0), pipeline_mode=pl.Buffered(1)),
                  pl.BlockSpec(kside.shape, lambda b, i: (0, 0, 0), pipeline_mode=pl.Buffered(1)),
                  pl.BlockSpec(qside.shape, lambda b, i: (0, 0, 0), pipeline_mode=pl.Buffered(1)),
                  pl.BlockSpec((1, tq, A_W), lambda b, i: (b, i, 0)),
                  pl.BlockSpec((1, t, A_W), lambda b, i: (b, 0, 1)),
                  pl.BlockSpec((1, t, A_W), lambda b, i: (b, 0, 2))],
        out_specs=pl.BlockSpec((1, tq, A_W), lambda b, i: (b, i, 0)),
        out_shape=jax.ShapeDtypeStruct((bsz, t, A_W), BF16),
        scratch_shapes=[pltpu.VMEM((nmap, t, LANES), BF16),
                        pltpu.VMEM((A_HEADS, t, LANES), BF16),
                        pltpu.VMEM((nmap, 3, tq, LANES), BF16),
                        pltpu.VMEM((tq, t), F32),
                        pltpu.VMEM((tq, t), F32),
                        pltpu.VMEM((nmap, tq, LANES), F32)],
        compiler_params=pltpu.CompilerParams(dimension_semantics=("parallel", "arbitrary"),
                                             vmem_limit_bytes=VMEM_LIMIT),
        name="attn_diff",
    )(lam_l, g2, tab, kside, qside, proj, proj, proj)


def _attn_b_kernel(sl_ref, q_ref, k_ref, v_ref, o_ref, qf, kf, vf, q4, k4, v4, m_st, l_st, o_st, bias_sc,
                   *, seq, tq, win, unroll):
    lane = _lane_iota((1, LANES))
    first = lane < HEAD_DIM
    halves = (first, jnp.logical_not(first))
    chunk = 512

    def prep(c, carry):
        r0 = pl.multiple_of(c * chunk, chunk)
        rows = pl.ds(r0, chunk)
        qf[rows, :] = q_ref[0, rows, :].astype(F32)
        kf[rows, :] = k_ref[0, rows, :].astype(F32)
        vf[rows, :] = v_ref[0, rows, :].astype(F32)
        return carry

    lax.fori_loop(0, seq // chunk, prep, 0)

    len4 = seq // 4
    for src_ref, dst_ref in ((qf, q4), (kf, k4), (vf, v4)):
        for r4 in range(4):
            for c0 in range(0, len4, chunk):
                dst_ref[r4 * len4 + c0:r4 * len4 + c0 + chunk, :] = src_ref[pl.ds(r4 + 4 * c0, chunk, stride=4), :]

    relb = (lax.broadcasted_iota(jnp.int32, (tq, win), 0) - lax.broadcasted_iota(jnp.int32, (tq, win), 1))

    for d in B_DILATIONS:
        length = seq // d
        nblk = length // tq

        for case in range(3):
            adist = jnp.abs(relb + case * B_HALF)
            adf = adist.astype(F32)
            for a in range(2):
                sl = sl_ref[0, a:a + 1, 0:1] * float(d)
                bias_sc[case, a] = jnp.where(adist <= B_HALF, -sl * adf, NEG)

        def state_rows(res, j, size, d=d):
            return pl.ds(j, size) if d == 1 else pl.ds(res + j * d, size, stride=d)

        def data_rows(res, j, size, d=d):
            if d == 1:
                return pl.ds(j, size)
            if d == 4:
                return pl.ds(res * len4 + j, size)
            return pl.ds((res % 4) * len4 + res // 4 + j * (d // 4), size, stride=d // 4)

        q_src, k_src, v_src = (qf, kf, vf) if d == 1 else (q4, k4, v4)

        merge = d != B_DILATIONS[0]

        def body(it, carry, length=length, nblk=nblk, state_rows=state_rows, data_rows=data_rows,
                 q_src=q_src, k_src=k_src, v_src=v_src, merge=merge):
            rqs, cases, qs, ks, vs = [], [], [], [], []
            for u in range(unroll):
                idx = it * unroll + u
                res = idx // nblk
                j0 = (idx % nblk) * tq
                start = jnp.clip(j0 - B_HALF, 0, length - win)
                rqs.append(state_rows(res, j0, tq))
                cases.append((j0 - start) // B_HALF)
                qs.append(q_src[data_rows(res, j0, tq), :])
                ks.append(k_src[data_rows(res, start, win), :].astype(BF16))
                vs.append(v_src[data_rows(res, start, win), :])
            items = [(u, a) for u in range(unroll) for a in range(2)]
            ss = [_nt_dot(jnp.where(halves[a], qs[u], 0.0).astype(BF16), ks[u]) for u, a in items]
            ts = [ss[i] + bias_sc[cases[u], a] for i, (u, a) in enumerate(items)]
            mbs = [jnp.max(t, axis=1, keepdims=True) for t in ts]
            prs = [jnp.exp2(t - mb).astype(BF16) for t, mb in zip(ts, mbs)]
            pvs = [jnp.dot(prs[i], jnp.where(halves[a], vs[u], 1.0).astype(BF16), preferred_element_type=F32)
                   for i, (u, a) in enumerate(items)]
            if merge:
                m_prev = [m_st[rq, :] for rq in rqs]
                l_prev = [l_st[rq, :] for rq in rqs]
                o_prev = [o_st[rq, :] for rq in rqs]
            for u in range(unroll):
                pv0, pv1 = pvs[2 * u], pvs[2 * u + 1]
                o_blk = jnp.where(first, pv0, pv1)
                l_blk = pltpu.roll(jnp.where(first, pv1, pv0), HEAD_DIM, 1)
                m_blk = jnp.where(first, mbs[2 * u], mbs[2 * u + 1])
                if not merge:
                    l_st[rqs[u], :] = l_blk
                    o_st[rqs[u], :] = o_blk
                    m_st[rqs[u], :] = jnp.broadcast_to(m_blk, (tq, LANES))
                    continue
                m_new = jnp.maximum(m_prev[u], m_blk)
                a_prev = jnp.exp2(m_prev[u] - m_new)
                a_blk = jnp.exp2(m_blk - m_new)
                l_st[rqs[u], :] = a_prev * l_prev[u] + a_blk * l_blk
                o_st[rqs[u], :] = a_prev * o_prev[u] + a_blk * o_blk
                m_st[rqs[u], :] = m_new
            return carry

        lax.fori_loop(0, seq // tq // unroll, body, 0)

    def fin(c, carry):
        r0 = pl.multiple_of(c * chunk, chunk)
        rows = pl.ds(r0, chunk)
        o_ref[0, rows, :] = (o_st[rows, :] / l_st[rows, :]).astype(o_ref.dtype)
        return carry

    lax.fori_loop(0, seq // chunk, fin, 0)


def _attention_b(proj, tq=128, win=256, unroll=32):
    bsz, t, _ = proj.shape
    npair = B_HEADS // 2
    sl = np.zeros((npair, 8, LANES), np.float32)
    for h, s in enumerate(SLOPES_B):
        sl[h // 2, h % 2, :] = s * LOG2E
    qb0 = 3 * A_W // LANES
    kern = functools.partial(_attn_b_kernel, seq=t, tq=tq, win=win, unroll=unroll)
    seq_spec = lambda off: pl.BlockSpec((1, t, LANES), lambda b, p, off=off: (b, 0, off + p))
    return pl.pallas_call(
        kern,
        grid=(bsz, npair),
        in_specs=[pl.BlockSpec((1, 8, LANES), lambda b, p: (p, 0, 0)),
                  seq_spec(qb0), seq_spec(qb0 + B_W // LANES), seq_spec(qb0 + 2 * B_W // LANES)],
        out_specs=pl.BlockSpec((1, t, LANES), lambda b, p: (b, 0, p)),
        out_shape=jax.ShapeDtypeStruct((bsz, t, B_W), BF16),
        scratch_shapes=[pltpu.VMEM((t, LANES), F32) for _ in range(9)] + [pltpu.VMEM((3, 2, tq, win), F32)],
        compiler_params=pltpu.CompilerParams(dimension_semantics=("parallel", "arbitrary"),
                                             vmem_limit_bytes=VMEM_LIMIT),
        name="attn_dilated",
    )(jnp.asarray(sl), proj, proj, proj)


def _attn_c_tables(tq, win):
    rel = np.arange(tq)[:, None] - np.arange(win)[None, :]
    tabs = []
    for case in range(3):
        dist = np.abs(rel + case * C_HALF_WINDOW).astype(np.float64)
        tabs.append(np.stack([np.where(dist <= C_HALF_WINDOW, -s * LOG2E * dist, NEG) for s in SLOPES_C]))
    return jnp.asarray(np.stack(tabs).astype(np.float32))


def _attn_c_kernel(sink_ref, tab_ref, q_ref, k_ref, v_ref, o_ref, *, seq, tq, win, nsub):
    qi = pl.program_id(1)
    lane = _lane_iota((1, LANES))
    first = lane < HEAD_DIM
    halves = (first, jnp.logical_not(first))
    rep = C_HEADS // C_KV_HEADS
    kv_of = [h // rep for h in range(C_HEADS)]
    items = [(sb, h) for sb in range(nsub) for h in range(C_HEADS)]
    cases, k_same, k_swap, v_aug = [], [], [], []
    for sb in range(nsub):
        i0 = (qi * nsub + sb) * tq
        start = pl.multiple_of(jnp.clip(i0 - C_HALF_WINDOW, 0, seq - win), C_HALF_WINDOW)
        cases.append((i0 - start) // C_HALF_WINDOW)
        k2 = k_ref[0, pl.ds(start, win), :].astype(F32)
        v2 = v_ref[0, pl.ds(start, win), :]
        k_same.append(k2.astype(BF16))
        k_swap.append(pltpu.roll(k2, HEAD_DIM, 1).astype(BF16))
        v_aug.append([jnp.where(halves[g], v2, jnp.ones_like(v2)) for g in range(C_KV_HEADS)])

    def q_masked(sb, h):
        qp = q_ref[0, sb * tq:(sb + 1) * tq, (h // 2) * LANES:(h // 2 + 1) * LANES]
        return jnp.where(halves[h % 2], qp, jnp.zeros_like(qp))

    ss = [_nt_dot(q_masked(sb, h), k_same[sb] if h % 2 == kv_of[h] else k_swap[sb]) for sb, h in items]
    ts = [ss[i] + tab_ref[cases[sb], h] for i, (sb, h) in enumerate(items)]
    ms = [jnp.max(t, axis=1, keepdims=True) for t in ts]
    prs = [jnp.exp2(t - m).astype(BF16) for t, m in zip(ts, ms)]
    pvs = [jnp.dot(prs[i], v_aug[sb][kv_of[h]], preferred_element_type=F32) for i, (sb, h) in enumerate(items)]
    swp = [pltpu.roll(pv, HEAD_DIM, 1) for pv in pvs]
    outs = [pvs[i] if h % 2 == kv_of[h] else swp[i] for i, (sb, h) in enumerate(items)]
    dens = [swp[i] if h % 2 == kv_of[h] else pvs[i] for i, (sb, h) in enumerate(items)]
    for sb in range(nsub):
        for p in range(C_HEADS // 2):
            i0, i1 = sb * C_HEADS + 2 * p, sb * C_HEADS + 2 * p + 1
            m_pair = jnp.where(first, ms[i0], ms[i1])
            l_pair = jnp.where(first, dens[i0], dens[i1])
            o_pair = jnp.where(first, outs[i0], outs[i1])
            sk = sink_ref[:, p * LANES:(p + 1) * LANES] * LOG2E
            mx = jnp.maximum(m_pair, sk)
            a_ = jnp.exp2(m_pair - mx)
            out = o_pair * a_ / (l_pair * a_ + jnp.exp2(sk - mx))
            o_ref[0, sb * tq:(sb + 1) * tq, p * LANES:(p + 1) * LANES] = out.astype(o_ref.dtype)


def _attention_c(proj, sink_lanes, tq=128, win=384, nsub=8):
    bsz, t, _ = proj.shape
    tab = _attn_c_tables(tq, win)
    kern = functools.partial(_attn_c_kernel, seq=t, tq=tq, win=win, nsub=nsub)
    qc0 = (3 * A_W + 3 * B_W) // C_QW
    kc0 = (3 * A_W + 3 * B_W + C_QW) // C_KW
    return pl.pallas_call(
        kern,
        grid=(bsz, t // (tq * nsub)),
        in_specs=[pl.BlockSpec((1, C_QW), lambda b, i: (0, 0)),
                  pl.BlockSpec(tab.shape, lambda b, i: (0, 0, 0, 0), pipeline_mode=pl.Buffered(1)),
                  pl.BlockSpec((1, tq * nsub, C_QW), lambda b, i: (b, i, qc0)),
                  pl.BlockSpec((1, t, C_KW), lambda b, i: (b, 0, kc0)),
                  pl.BlockSpec((1, t, C_KW), lambda b, i: (b, 0, kc0 + 1))],
        out_specs=pl.BlockSpec((1, tq * nsub, C_QW), lambda b, i: (b, i, 0)),
        out_shape=jax.ShapeDtypeStruct((bsz, t, C_QW), BF16),
        compiler_params=pltpu.CompilerParams(dimension_semantics=("parallel", "arbitrary"),
                                             vmem_limit_bytes=VMEM_LIMIT),
        name="attn_window",
    )(sink_lanes, tab, proj, proj, proj)


def _layer_norm(y, g, b):
    mu = jnp.mean(y, axis=-1, keepdims=True)
    yc = y - mu
    var = jnp.mean(yc * yc, axis=-1, keepdims=True)
    return yc * lax.rsqrt(var + LN_EPS) * g + b


def _tail_kernel(oa_ref, ob_ref, oc_ref, x_ref, mod_ref, wo_ref, wgu_ref, wd_ref, lng_ref, lnb_ref, o_ref,
                 *, alpha, hidden, ck):
    mix = (jnp.dot(oa_ref[0], wo_ref[0:A_W, :], preferred_element_type=F32)
           + jnp.dot(ob_ref[0], wo_ref[A_W:A_W + B_W, :], preferred_element_type=F32)
           + jnp.dot(oc_ref[0], wo_ref[A_W + B_W:, :], preferred_element_type=F32))
    g1 = mod_ref[0, 2:3, :]
    x1 = _layer_norm(alpha * x_ref[0] + (1.0 + g1) * mix, lng_ref[0:1, :], lnb_ref[0:1, :])
    sh2 = mod_ref[0, 3:4, :]
    sc2 = mod_ref[0, 4:5, :]
    g2 = mod_ref[0, 5:6, :]
    h2 = (x1 * (1.0 + sc2) + sh2).astype(BF16)
    acc = jnp.zeros(x1.shape, F32)
    for c0 in range(0, hidden, ck):
        gate = jnp.dot(h2, wgu_ref[:, c0:c0 + ck], preferred_element_type=F32)
        up = jnp.dot(h2, wgu_ref[:, hidden + c0:hidden + c0 + ck], preferred_element_type=F32)
        act = (gate * jax.nn.sigmoid(gate) * up).astype(BF16)
        acc = acc + jnp.dot(act, wd_ref[c0:c0 + ck, :], preferred_element_type=F32)
    o_ref[0] = _layer_norm(alpha * x1 + (1.0 + g2) * acc, lng_ref[1:2, :], lnb_ref[1:2, :])


def _tail(oa, ob, oc, x, mod6, wo_bf, wgu_bf, wd_bf, layer, ln_g, ln_b, alpha, tm=512):
    bsz, t, d = x.shape
    hidden = wd_bf.shape[1]
    kern = functools.partial(_tail_kernel, alpha=alpha, hidden=hidden, ck=256)
    row = lambda w: pl.BlockSpec((1, tm, w), lambda b, i: (b, i, 0))
    whole = lambda shp: pl.BlockSpec((None,) + tuple(shp[1:]), lambda b, i: (layer, 0, 0),
                                     pipeline_mode=pl.Buffered(1))
    return pl.pallas_call(
        kern,
        grid=(bsz, t // tm),
        in_specs=[row(A_W), row(B_W), row(C_QW), row(d),
                  pl.BlockSpec((1, 6, d), lambda b, i: (b, 0, 0)),
                  whole(wo_bf.shape), whole(wgu_bf.shape), whole(wd_bf.shape),
                  pl.BlockSpec((2, d), lambda b, i: (0, 0)),
                  pl.BlockSpec((2, d), lambda b, i: (0, 0))],
        out_specs=row(d),
        out_shape=jax.ShapeDtypeStruct((bsz, t, d), F32),
        compiler_params=pltpu.CompilerParams(dimension_semantics=("parallel", "arbitrary"),
                                             vmem_limit_bytes=VMEM_LIMIT),
        name="outproj_ffn",
    )(oa, ob, oc, x, mod6, wo_bf, wgu_bf, wd_bf, ln_g, ln_b)


def _q_col_scale():
    cs = np.ones((1, IN_WIDTH), np.float32)
    cs[0, 0:A_W] = A_QK_DIM ** -0.5 * LOG2E
    cs[0, 3 * A_W:3 * A_W + B_W] = HEAD_DIM ** -0.5 * LOG2E
    cs[0, 3 * A_W + 3 * B_W:3 * A_W + 3 * B_W + C_QW] = HEAD_DIM ** -0.5 * LOG2E
    return jnp.asarray(cs)


def kernel(x, c, w_ada, b_ada, w_in, lam, subln_g, sink, w_out, ln_g, ln_b, w_gu, w_down):
    depth = w_ada.shape[0]
    bsz, _, d = x.shape
    alpha = (2 * depth) ** 0.25
    mod = _modulation(c, w_ada, b_ada).reshape(depth, bsz, 6, d)
    col_scale = _q_col_scale()
    w_in_bf, w_out_bf, w_gu_bf, w_down_bf = (w.astype(BF16) for w in (w_in, w_out, w_gu, w_down))
    for layer in range(depth):
        lambda_init = 0.8 - 0.6 * math.exp(-0.3 * layer)
        proj = _in_projection(x, mod[layer], w_in_bf, layer, col_scale)
        g2 = jnp.tile(subln_g[layer], 2).reshape(1, LANES)
        oa = _attention_a(proj, lam[layer], g2, lambda_init)
        ob = _attention_b(proj)
        oc = _attention_c(proj, jnp.repeat(sink[layer], HEAD_DIM).reshape(1, C_QW))
        x = _tail(oa, ob, oc, x, mod[layer], w_out_bf, w_gu_bf, w_down_bf, layer, ln_g[layer], ln_b[layer], alpha)
    return x
```

```python
import functools
import math

import numpy as np
import jax
import jax.numpy as jnp
from jax import lax
from jax.experimental import pallas as pl
from jax.experimental.pallas import tpu as pltpu

F32 = jnp.float32
BF16 = jnp.bfloat16

HEAD_DIM = 64
A_HEADS = 4
A_QK_DIM = 32
B_HEADS = 6
B_DILATIONS = (16, 4, 1)
B_HALF = 64
C_HEADS = 6
C_KV_HEADS = 2
C_HALF_WINDOW = 128
N_ATTN_HEADS = 16
LN_EPS = 1e-5
NEG = -1e30
LOG2E = math.log2(math.e)

LANES = 128
A_W = A_HEADS * HEAD_DIM
B_W = B_HEADS * HEAD_DIM
C_QW = C_HEADS * HEAD_DIM
C_KW = C_KV_HEADS * HEAD_DIM
IN_WIDTH = 3 * A_W + 3 * B_W + C_QW + 2 * C_KW
VMEM_LIMIT = 56 * 1024 * 1024

_S = 2.0 ** (-8.0 * np.arange(1, N_ATTN_HEADS + 1) / N_ATTN_HEADS)
SLOPES_C = [float(v) for v in _S[:C_HEADS]]
SLOPES_A = [float(v) for v in _S[C_HEADS:C_HEADS + A_HEADS]]
SLOPES_B = [float(v) for v in _S[C_HEADS + A_HEADS:]]


def _nt_dot(a, b):
    return lax.dot_general(a, b, (((1,), (1,)), ((), ())), preferred_element_type=F32)


def _lane_iota(shape):
    return lax.broadcasted_iota(jnp.int32, shape, len(shape) - 1)


def _mod_kernel(c_ref, w_ref, b_ref, o_ref):
    c = c_ref[...]
    a = (c * jax.nn.sigmoid(c)).astype(BF16)
    o_ref[0] = jnp.dot(a, w_ref[0].astype(BF16), preferred_element_type=F32) + b_ref[0]


def _modulation(c, w_ada, b_ada):
    depth, d, n = w_ada.shape
    bsz = c.shape[0]
    tn = 1536
    return pl.pallas_call(
        _mod_kernel,
        grid=(depth, n // tn),
        in_specs=[pl.BlockSpec((bsz, d), lambda l, j: (0, 0)),
                  pl.BlockSpec((1, d, tn), lambda l, j: (l, 0, j)),
                  pl.BlockSpec((1, 1, tn), lambda l, j: (l, 0, j))],
        out_specs=pl.BlockSpec((1, bsz, tn), lambda l, j: (l, 0, j)),
        out_shape=jax.ShapeDtypeStruct((depth, bsz, n), F32),
        compiler_params=pltpu.CompilerParams(dimension_semantics=("arbitrary", "arbitrary"),
                                             vmem_limit_bytes=VMEM_LIMIT),
        name="adaln_mod",
    )(c, w_ada, b_ada.reshape(depth, 1, n))


def _proj_kernel(x_ref, mod_ref, w_ref, cs_ref, o_ref, *, tn):
    sh = mod_ref[0, 0:1, :]
    sc = mod_ref[0, 1:2, :]
    h = (x_ref[0] * (1.0 + sc) + sh).astype(BF16)
    for n0 in range(0, o_ref.shape[2], tn):
        acc = jnp.dot(h, w_ref[:, n0:n0 + tn], preferred_element_type=F32)
        o_ref[0, :, n0:n0 + tn] = (acc * cs_ref[:, n0:n0 + tn]).astype(BF16)


def _in_projection(x, mod6, w_in_bf, layer, col_scale, tm=512):
    bsz, t, d = x.shape
    n = w_in_bf.shape[2]
    return pl.pallas_call(
        functools.partial(_proj_kernel, tn=512),
        grid=(bsz, t // tm),
        in_specs=[pl.BlockSpec((1, tm, d), lambda b, i: (b, i, 0)),
                  pl.BlockSpec((1, 6, d), lambda b, i: (b, 0, 0)),
                  pl.BlockSpec((None, d, n), lambda b, i: (layer, 0, 0), pipeline_mode=pl.Buffered(1)),
                  pl.BlockSpec((1, n), lambda b, i: (0, 0))],
        out_specs=pl.BlockSpec((1, tm, n), lambda b, i: (b, i, 0)),
        out_shape=jax.ShapeDtypeStruct((bsz, t, n), BF16),
        compiler_params=pltpu.CompilerParams(dimension_semantics=("parallel", "arbitrary"),
                                             vmem_limit_bytes=VMEM_LIMIT),
        name="in_proj",
    )(x, mod6, w_in_bf, col_scale)


def _attn_a_tables(tq, tk):
    rel = (np.arange(tq)[:, None] - np.arange(tk)[None, :]).astype(np.float64)
    tabs = [np.stack([-s * LOG2E * np.abs(rel + c * tq) for c in range(tk // tq)]) for s in SLOPES_A]
    return jnp.asarray(np.stack(tabs).astype(np.float32))


def _alibi_lane(head):
    return 3 * A_QK_DIM if head % 2 == 0 else 0


def _attn_a_alibi_lanes(seq):
    pos = np.arange(seq, dtype=np.float64)
    bf = jnp.bfloat16
    kside = np.zeros((A_HEADS, seq, LANES), np.float32)
    qside = np.zeros((A_HEADS, seq, LANES), np.float32)
    for h, s in enumerate(SLOPES_A):
        a0 = _alibi_lane(h)
        x = (pos * (s * LOG2E)).astype(np.float32)
        parts, rest = [], x
        for _ in range(3):
            part = np.asarray(rest.astype(bf), np.float32)
            parts.append(part)
            rest = (rest - part).astype(np.float32)
        for i, part in enumerate(parts):
            kside[h, :, a0 + i] = part
            qside[h, :, a0 + 3 + i] = -part
        kside[h, :, a0 + 3:a0 + 6] = 1.0
        qside[h, :, a0:a0 + 3] = 1.0
    return jnp.asarray(kside, BF16), jnp.asarray(qside, BF16)


def _attn_a_kernel(lam_ref, g_ref, tab_ref, kside_ref, qside_ref, q_ref, k_ref, v_ref, o_ref,
                   kaug, vaug, qv_sc, t_even, t_odd, acc_sc, *, tq, tk, seq, lambda_init):
    qi = pl.program_id(1)
    r = tk // tq
    nkv = seq // tk
    kd = qi // r
    var_d = qi % r
    i0 = pl.multiple_of(qi * tq, tq)
    lane = _lane_iota((1, LANES))
    first = lane < HEAD_DIM
    nmap = 2 * A_HEADS
    nlt = tk // LANES
    rows_per_build = 512

    def group_lanes(gm):
        lo = A_QK_DIM * (gm % 4)
        return (lane >= lo) & (lane < lo + A_QK_DIM)

    @pl.when(qi == 0)
    def _build_kv_copies():
        for rc in range(seq // rows_per_build):
            rows = slice(rc * rows_per_build, (rc + 1) * rows_per_build)
            for p in range(A_HEADS // 2):
                kp = k_ref[0, rows, p * LANES:(p + 1) * LANES]
                vp = v_ref[0, rows, p * LANES:(p + 1) * LANES]
                for a in range(2):
                    vaug[2 * p + a, rows, :] = jnp.where(first if a == 0 else jnp.logical_not(first), vp,
                                                         jnp.ones_like(vp))
                for g in range(4):
                    gm = 4 * p + g
                    kaug[gm, rows, :] = jnp.where(group_lanes(gm), kp, kside_ref[gm // 2, rows, :])

    for p in range(A_HEADS // 2):
        qp = q_ref[0, :, p * LANES:(p + 1) * LANES]
        for g in range(4):
            gm = 4 * p + g
            side = qside_ref[gm // 2, pl.ds(i0, tq), :]
            qv_sc[gm, 0] = jnp.where(group_lanes(gm), qp, jnp.zeros_like(qp))
            qv_sc[gm, 1] = jnp.where(group_lanes(gm), qp, side)
            qv_sc[gm, 2] = jnp.where(group_lanes(gm), qp, -side)

    chunk_ids = [kd] + [(kd + cc) & (nkv - 1) for cc in range(1, nkv)]
    starts = [pl.multiple_of(cid * tk, tk) for cid in chunk_ids]
    q_variant = [0] + [jnp.where(cid < kd, 1, 2) for cid in chunk_ids[1:]]

    def logits_chunk(gm, cc, mx):
        t_sc = t_odd if gm % 2 else t_even
        s = _nt_dot(qv_sc[gm, q_variant[cc]], kaug[gm, pl.ds(starts[cc], tk), :])
        if cc == 0:
            s = s + tab_ref[gm // 2, var_d]
        t_sc[:, cc * tk:(cc + 1) * tk] = s
        cm = s[:, 0:LANES]
        for lt in range(1, nlt):
            cm = jnp.maximum(cm, s[:, lt * LANES:(lt + 1) * LANES])
        return cm if mx is None else jnp.maximum(mx, cm)

    def row_max(mx):
        return jnp.broadcast_to(jnp.max(mx, axis=1, keepdims=True), (tq, LANES))

    def pv_chunk(gm, cc, mb, acc):
        t_sc = t_odd if gm % 2 else t_even
        blocks = []
        for rb in range(0, tq, 64):
            blocks.append(jnp.concatenate(
                [jnp.exp2(t_sc[rb:rb + 64, cc * tk + lt * LANES:cc * tk + (lt + 1) * LANES]
                          - mb[rb:rb + 64, :]).astype(BF16) for lt in range(nlt)], axis=1))
        pv = jnp.dot(jnp.concatenate(blocks, axis=0), vaug[gm // 2, pl.ds(starts[cc], tk), :],
                     preferred_element_type=F32)
        return pv if acc is None else acc + pv

    mx = None
    for cc in range(nkv):
        mx = logits_chunk(0, cc, mx)
    mb = row_max(mx)
    for gm in range(nmap):
        mx, acc = None, None
        for cc in range(nkv):
            if gm + 1 < nmap:
                mx = logits_chunk(gm + 1, cc, mx)
            acc = pv_chunk(gm, cc, mb, acc)
        acc_sc[gm] = acc
        if gm + 1 < nmap:
            mb = row_max(mx)

    lf = lam_ref[...]
    lam_full = (jnp.exp(jnp.sum(lf[0:1] * lf[1:2], axis=(0, 1), keepdims=True))
                - jnp.exp(jnp.sum(lf[2:3] * lf[3:4], axis=(0, 1), keepdims=True)) + lambda_init)
    for p in range(A_HEADS // 2):
        d_heads = []
        for a in range(2):
            o_m = []
            for m2 in range(2):
                acc = acc_sc[4 * p + 2 * a + m2]
                o_m.append(acc / pltpu.roll(acc, HEAD_DIM, 1))
            d_heads.append(o_m[0] - lam_full * o_m[1])
        dp = jnp.where(first, d_heads[0], d_heads[1])
        sq = dp * dp
        ms0 = jnp.sum(jnp.where(first, sq, 0.0), axis=1, keepdims=True) * (1.0 / HEAD_DIM)
        ms1 = jnp.sum(jnp.where(first, 0.0, sq), axis=1, keepdims=True) * (1.0 / HEAD_DIM)
        ms = jnp.where(first, ms0, ms1)
        out = dp * lax.rsqrt(ms + LN_EPS) * g_ref[...] * (1.0 - lambda_init)
        o_ref[0, :, p * LANES:(p + 1) * LANES] = out.astype(o_ref.dtype)


def _attention_a(proj, lam_l, g2, lambda_init, tq=512, tk=512):
    bsz, t, _ = proj.shape
    tab = _attn_a_tables(tq, tk)
    kside, qside = _attn_a_alibi_lanes(t)
    nmap = 2 * A_HEADS
    kern = functools.partial(_attn_a_kernel, tq=tq, tk=tk, seq=t, lambda_init=lambda_init)
    return pl.pallas_call(
        kern,
        grid=(bsz, t // tq),
        in_specs=[pl.BlockSpec((4, A_QK_DIM), lambda b, i: (0, 0)),
                  pl.BlockSpec((1, LANES), lambda b, i: (0, 0)),
                  pl.BlockSpec(tab.shape, lambda b, i: (0, 0, 0, 0), pipeline_mode=pl.Buffered(1)),
                  pl.BlockSpec(kside.shape, lambda b, i: (0, 0, 0), pipeline_mode=pl.Buffered(1)),
                  pl.BlockSpec(qside.shape, lambda b, i: (0, 0, 0), pipeline_mode=pl.Buffered(1)),
                  pl.BlockSpec((1, tq, A_W), lambda b, i: (b, i, 0)),
                  pl.BlockSpec((1, t, A_W), lambda b, i: (b, 0, 1), pipeline_mode=pl.Buffered(1)),
                  pl.BlockSpec((1, t, A_W), lambda b, i: (b, 0, 2), pipeline_mode=pl.Buffered(1))],
        out_specs=pl.BlockSpec((1, tq, A_W), lambda b, i: (b, i, 0)),
        out_shape=jax.ShapeDtypeStruct((bsz, t, A_W), BF16),
        scratch_shapes=[pltpu.VMEM((nmap, t, LANES), BF16),
                        pltpu.VMEM((A_HEADS, t, LANES), BF16),
                        pltpu.VMEM((nmap, 3, tq, LANES), BF16),
                        pltpu.VMEM((tq, t), F32),
                        pltpu.VMEM((tq, t), F32),
                        pltpu.VMEM((nmap, tq, LANES), F32)],
        compiler_params=pltpu.CompilerParams(dimension_semantics=("parallel", "arbitrary"),
                                             vmem_limit_bytes=VMEM_LIMIT),
        name="attn_diff",
    )(lam_l, g2, tab, kside, qside, proj, proj, proj)


def _attn_b_kernel(sl_ref, q_ref, k_ref, v_ref, o_ref, qf, kf, vf, q4, k4, v4, m_st, l_st, o_st, bias_sc,
                   *, seq, tq, win, unroll):
    lane = _lane_iota((1, LANES))
    first = lane < HEAD_DIM
    halves = (first, jnp.logical_not(first))
    chunk = 512

    def prep(c, carry):
        r0 = pl.multiple_of(c * chunk, chunk)
        rows = pl.ds(r0, chunk)
        qf[rows, :] = q_ref[0, rows, :].astype(F32)
        kf[rows, :] = k_ref[0, rows, :].astype(F32)
        vf[rows, :] = v_ref[0, rows, :].astype(F32)
        return carry

    lax.fori_loop(0, seq // chunk, prep, 0)

    len4 = seq // 4
    for src_ref, dst_ref in ((qf, q4), (kf, k4), (vf, v4)):
        for r4 in range(4):
            for c0 in range(0, len4, chunk):
                dst_ref[r4 * len4 + c0:r4 * len4 + c0 + chunk, :] = src_ref[pl.ds(r4 + 4 * c0, chunk, stride=4), :]

    relb = (lax.broadcasted_iota(jnp.int32, (tq, win), 0) - lax.broadcasted_iota(jnp.int32, (tq, win), 1))

    for d in B_DILATIONS:
        length = seq // d
        nblk = length // tq

        for case in range(3):
            adist = jnp.abs(relb + case * B_HALF)
            adf = adist.astype(F32)
            for a in range(2):
                sl = sl_ref[0, a:a + 1, 0:1] * float(d)
                bias_sc[case, a] = jnp.where(adist <= B_HALF, -sl * adf, NEG)

        def state_rows(res, j, size, d=d):
            return pl.ds(j, size) if d == 1 else pl.ds(res + j * d, size, stride=d)

        def data_rows(res, j, size, d=d):
            if d == 1:
                return pl.ds(j, size)
            if d == 4:
                return pl.ds(res * len4 + j, size)
            return pl.ds((res % 4) * len4 + res // 4 + j * (d // 4), size, stride=d // 4)

        q_src, k_src, v_src = (qf, kf, vf) if d == 1 else (q4, k4, v4)

        merge = d != B_DILATIONS[0]

        def body(it, carry, length=length, nblk=nblk, state_rows=state_rows, data_rows=data_rows,
                 q_src=q_src, k_src=k_src, v_src=v_src, merge=merge):
            rqs, cases, qs, ks, vs = [], [], [], [], []
            for u in range(unroll):
                idx = it * unroll + u
                res = idx // nblk
                j0 = (idx % nblk) * tq
                start = jnp.clip(j0 - B_HALF, 0, length - win)
                rqs.append(state_rows(res, j0, tq))
                cases.append((j0 - start) // B_HALF)
                qs.append(q_src[data_rows(res, j0, tq), :])
                ks.append(k_src[data_rows(res, start, win), :].astype(BF16))
                vs.append(v_src[data_rows(res, start, win), :])
            items = [(u, a) for u in range(unroll) for a in range(2)]
            ss = [_nt_dot(jnp.where(halves[a], qs[u], 0.0).astype(BF16), ks[u]) for u, a in items]
            ts = [ss[i] + bias_sc[cases[u], a] for i, (u, a) in enumerate(items)]
            mbs = [jnp.max(t, axis=1, keepdims=True) for t in ts]
            prs = [jnp.exp2(t - mb).astype(BF16) for t, mb in zip(ts, mbs)]
            pvs = [jnp.dot(prs[i], jnp.where(halves[a], vs[u], 1.0).astype(BF16), preferred_element_type=F32)
                   for i, (u, a) in enumerate(items)]
            if merge:
                m_prev = [m_st[rq, :] for rq in rqs]
                l_prev = [l_st[rq, :] for rq in rqs]
                o_prev = [o_st[rq, :] for rq in rqs]
            for u in range(unroll):
                pv0, pv1 = pvs[2 * u], pvs[2 * u + 1]
                o_blk = jnp.where(first, pv0, pv1)
                l_blk = pltpu.roll(jnp.where(first, pv1, pv0), HEAD_DIM, 1)
                m_blk = jnp.where(first, mbs[2 * u], mbs[2 * u + 1])
                if not merge:
                    l_st[rqs[u], :] = l_blk
                    o_st[rqs[u], :] = o_blk
                    m_st[rqs[u], :] = jnp.broadcast_to(m_blk, (tq, LANES))
                    continue
                m_new = jnp.maximum(m_prev[u], m_blk)
                a_prev = jnp.exp2(m_prev[u] - m_new)
                a_blk = jnp.exp2(m_blk - m_new)
                l_st[rqs[u], :] = a_prev * l_prev[u] + a_blk * l_blk
                o_st[rqs[u], :] = a_prev * o_prev[u] + a_blk * o_blk
                m_st[rqs[u], :] = m_new
            return carry

        lax.fori_loop(0, seq // tq // unroll, body, 0)

    def fin(c, carry):
        r0 = pl.multiple_of(c * chunk, chunk)
        rows = pl.ds(r0, chunk)
        o_ref[0, rows, :] = (o_st[rows, :] / l_st[rows, :]).astype(o_ref.dtype)
        return carry

    lax.fori_loop(0, seq // chunk, fin, 0)


def _attention_b(proj, tq=128, win=256, unroll=32):
    bsz, t, _ = proj.shape
    npair = B_HEADS // 2
    sl = np.zeros((npair, 8, LANES), np.float32)
    for h, s in enumerate(SLOPES_B):
        sl[h // 2, h % 2, :] = s * LOG2E
    qb0 = 3 * A_W // LANES
    kern = functools.partial(_attn_b_kernel, seq=t, tq=tq, win=win, unroll=unroll)
    seq_spec = lambda off: pl.BlockSpec((1, t, LANES), lambda b, p, off=off: (b, 0, off + p))
    return pl.pallas_call(
        kern,
        grid=(bsz, npair),
        in_specs=[pl.BlockSpec((1, 8, LANES), lambda b, p: (p, 0, 0)),
                  seq_spec(qb0), seq_spec(qb0 + B_W // LANES), seq_spec(qb0 + 2 * B_W // LANES)],
        out_specs=pl.BlockSpec((1, t, LANES), lambda b, p: (b, 0, p)),
        out_shape=jax.ShapeDtypeStruct((bsz, t, B_W), BF16),
        scratch_shapes=[pltpu.VMEM((t, LANES), F32) for _ in range(9)] + [pltpu.VMEM((3, 2, tq, win), F32)],
        compiler_params=pltpu.CompilerParams(dimension_semantics=("parallel", "arbitrary"),
                                             vmem_limit_bytes=VMEM_LIMIT),
        name="attn_dilated",
    )(jnp.asarray(sl), proj, proj, proj)


def _attn_c_tables(tq, win):
    rel = np.arange(tq)[:, None] - np.arange(win)[None, :]
    tabs = []
    for case in range(3):
        dist = np.abs(rel + case * C_HALF_WINDOW).astype(np.float64)
        tabs.append(np.stack([np.where(dist <= C_HALF_WINDOW, -s * LOG2E * dist, NEG) for s in SLOPES_C]))
    return jnp.asarray(np.stack(tabs).astype(np.float32))


def _attn_c_kernel(sink_ref, tab_ref, q_ref, k_ref, v_ref, o_ref, *, seq, tq, win, nsub):
    qi = pl.program_id(1)
    lane = _lane_iota((1, LANES))
    first = lane < HEAD_DIM
    halves = (first, jnp.logical_not(first))
    rep = C_HEADS // C_KV_HEADS
    kv_of = [h // rep for h in range(C_HEADS)]
    items = [(sb, h) for sb in range(nsub) for h in range(C_HEADS)]
    cases, k_same, k_swap, v_aug = [], [], [], []
    for sb in range(nsub):
        i0 = (qi * nsub + sb) * tq
        start = pl.multiple_of(jnp.clip(i0 - C_HALF_WINDOW, 0, seq - win), C_HALF_WINDOW)
        cases.append((i0 - start) // C_HALF_WINDOW)
        k2 = k_ref[0, pl.ds(start, win), :].astype(F32)
        v2 = v_ref[0, pl.ds(start, win), :]
        k_same.append(k2.astype(BF16))
        k_swap.append(pltpu.roll(k2, HEAD_DIM, 1).astype(BF16))
        v_aug.append([jnp.where(halves[g], v2, jnp.ones_like(v2)) for g in range(C_KV_HEADS)])

    def q_masked(sb, h):
        qp = q_ref[0, sb * tq:(sb + 1) * tq, (h // 2) * LANES:(h // 2 + 1) * LANES]
        return jnp.where(halves[h % 2], qp, jnp.zeros_like(qp))

    ss = [_nt_dot(q_masked(sb, h), k_same[sb] if h % 2 == kv_of[h] else k_swap[sb]) for sb, h in items]
    ts = [ss[i] + tab_ref[cases[sb], h] for i, (sb, h) in enumerate(items)]
    ms = [jnp.max(t, axis=1, keepdims=True) for t in ts]
    prs = [jnp.exp2(t - m).astype(BF16) for t, m in zip(ts, ms)]
    pvs = [jnp.dot(prs[i], v_aug[sb][kv_of[h]], preferred_element_type=F32) for i, (sb, h) in enumerate(items)]
    swp = [pltpu.roll(pv, HEAD_DIM, 1) for pv in pvs]
    outs = [pvs[i] if h % 2 == kv_of[h] else swp[i] for i, (sb, h) in enumerate(items)]
    dens = [swp[i] if h % 2 == kv_of[h] else pvs[i] for i, (sb, h) in enumerate(items)]
    for sb in range(nsub):
        for p in range(C_HEADS // 2):
            i0, i1 = sb * C_HEADS + 2 * p, sb * C_HEADS + 2 * p + 1
            m_pair = jnp.where(first, ms[i0], ms[i1])
            l_pair = jnp.where(first, dens[i0], dens[i1])
            o_pair = jnp.where(first, outs[i0], outs[i1])
            sk = sink_ref[:, p * LANES:(p + 1) * LANES] * LOG2E
            mx = jnp.maximum(m_pair, sk)
            a_ = jnp.exp2(m_pair - mx)
            out = o_pair * a_ / (l_pair * a_ + jnp.exp2(sk - mx))
            o_ref[0, sb * tq:(sb + 1) * tq, p * LANES:(p + 1) * LANES] = out.astype(o_ref.dtype)


def _attention_c(proj, sink_lanes, tq=128, win=384, nsub=8):
    bsz, t, _ = proj.shape
    tab = _attn_c_tables(tq, win)
    kern = functools.partial(_attn_c_kernel, seq=t, tq=tq, win=win, nsub=nsub)
    qc0 = (3 * A_W + 3 * B_W) // C_QW
    kc0 = (3 * A_W + 3 * B_W + C_QW) // C_KW
    return pl.pallas_call(
        kern,
        grid=(bsz, t // (tq * nsub)),
        in_specs=[pl.BlockSpec((1, C_QW), lambda b, i: (0, 0)),
                  pl.BlockSpec(tab.shape, lambda b, i: (0, 0, 0, 0), pipeline_mode=pl.Buffered(1)),
                  pl.BlockSpec((1, tq * nsub, C_QW), lambda b, i: (b, i, qc0)),
                  pl.BlockSpec((1, t, C_KW), lambda b, i: (b, 0, kc0)),
                  pl.BlockSpec((1, t, C_KW), lambda b, i: (b, 0, kc0 + 1))],
        out_specs=pl.BlockSpec((1, tq * nsub, C_QW), lambda b, i: (b, i, 0)),
        out_shape=jax.ShapeDtypeStruct((bsz, t, C_QW), BF16),
        compiler_params=pltpu.CompilerParams(dimension_semantics=("parallel", "arbitrary"),
                                             vmem_limit_bytes=VMEM_LIMIT),
        name="attn_window",
    )(sink_lanes, tab, proj, proj, proj)


def _layer_norm(y, g, b):
    mu = jnp.mean(y, axis=-1, keepdims=True)
    yc = y - mu
    var = jnp.mean(yc * yc, axis=-1, keepdims=True)
    return yc * lax.rsqrt(var + LN_EPS) * g + b


def _tail_kernel(oa_ref, ob_ref, oc_ref, x_ref, mod_ref, wo_ref, wgu_ref, wd_ref, lng_ref, lnb_ref, o_ref,
                 *, alpha, hidden, ck):
    mix = (jnp.dot(oa_ref[0], wo_ref[0:A_W, :], preferred_element_type=F32)
           + jnp.dot(ob_ref[0], wo_ref[A_W:A_W + B_W, :], preferred_element_type=F32)
           + jnp.dot(oc_ref[0], wo_ref[A_W + B_W:, :], preferred_element_type=F32))
    g1 = mod_ref[0, 2:3, :]
    x1 = _layer_norm(alpha * x_ref[0] + (1.0 + g1) * mix, lng_ref[0:1, :], lnb_ref[0:1, :])
    sh2 = mod_ref[0, 3:4, :]
    sc2 = mod_ref[0, 4:5, :]
    g2 = mod_ref[0, 5:6, :]
    h2 = (x1 * (1.0 + sc2) + sh2).astype(BF16)
    acc = jnp.zeros(x1.shape, F32)
    for c0 in range(0, hidden, ck):
        gate = jnp.dot(h2, wgu_ref[:, c0:c0 + ck], preferred_element_type=F32)
        up = jnp.dot(h2, wgu_ref[:, hidden + c0:hidden + c0 + ck], preferred_element_type=F32)
        act = (gate * jax.nn.sigmoid(gate) * up).astype(BF16)
        acc = acc + jnp.dot(act, wd_ref[c0:c0 + ck, :], preferred_element_type=F32)
    o_ref[0] = _layer_norm(alpha * x1 + (1.0 + g2) * acc, lng_ref[1:2, :], lnb_ref[1:2, :])


def _tail(oa, ob, oc, x, mod6, wo_bf, wgu_bf, wd_bf, layer, ln_g, ln_b, alpha, tm=512):
    bsz, t, d = x.shape
    hidden = wd_bf.shape[1]
    kern = functools.partial(_tail_kernel, alpha=alpha, hidden=hidden, ck=256)
    row = lambda w: pl.BlockSpec((1, tm, w), lambda b, i: (b, i, 0))
    whole = lambda shp: pl.BlockSpec((None,) + tuple(shp[1:]), lambda b, i: (layer, 0, 0),
                                     pipeline_mode=pl.Buffered(1))
    return pl.pallas_call(
        kern,
        grid=(bsz, t // tm),
        in_specs=[row(A_W), row(B_W), row(C_QW), row(d),
                  pl.BlockSpec((1, 6, d), lambda b, i: (b, 0, 0)),
                  whole(wo_bf.shape), whole(wgu_bf.shape), whole(wd_bf.shape),
                  pl.BlockSpec((2, d), lambda b, i: (0, 0)),
                  pl.BlockSpec((2, d), lambda b, i: (0, 0))],
        out_specs=row(d),
        out_shape=jax.ShapeDtypeStruct((bsz, t, d), F32),
        compiler_params=pltpu.CompilerParams(dimension_semantics=("parallel", "arbitrary"),
                                             vmem_limit_bytes=VMEM_LIMIT),
        name="outproj_ffn",
    )(oa, ob, oc, x, mod6, wo_bf, wgu_bf, wd_bf, ln_g, ln_b)


def _q_col_scale():
    cs = np.ones((1, IN_WIDTH), np.float32)
    cs[0, 0:A_W] = A_QK_DIM ** -0.5 * LOG2E
    cs[0, 3 * A_W:3 * A_W + B_W] = HEAD_DIM ** -0.5 * LOG2E
    cs[0, 3 * A_W + 3 * B_W:3 * A_W + 3 * B_W + C_QW] = HEAD_DIM ** -0.5 * LOG2E
    return jnp.asarray(cs)


def kernel(x, c, w_ada, b_ada, w_in, lam, subln_g, sink, w_out, ln_g, ln_b, w_gu, w_down):
    depth = w_ada.shape[0]
    bsz, _, d = x.shape
    alpha = (2 * depth) ** 0.25
    mod = _modulation(c, w_ada, b_ada).reshape(depth, bsz, 6, d)
    col_scale = _q_col_scale()
    w_in_bf, w_out_bf, w_gu_bf, w_down_bf = (w.astype(BF16) for w in (w_in, w_out, w_gu, w_down))
    for layer in range(depth):
        lambda_init = 0.8 - 0.6 * math.exp(-0.3 * layer)
        proj = _in_projection(x, mod[layer], w_in_bf, layer, col_scale)
        g2 = jnp.tile(subln_g[layer], 2).reshape(1, LANES)
        oa = _attention_a(proj, lam[layer], g2, lambda_init)
        ob = _attention_b(proj)
        oc = _attention_c(proj, jnp.repeat(sink[layer], HEAD_DIM).reshape(1, C_QW))
        x = _tail(oa, ob, oc, x, mod[layer], w_out_bf, w_gu_bf, w_down_bf, layer, ln_g[layer], ln_b[layer], alpha)
    return x
```

```python
import functools
import math

import numpy as np
import jax
import jax.numpy as jnp
from jax import lax
from jax.experimental import pallas as pl
from jax.experimental.pallas import tpu as pltpu

F32 = jnp.float32
BF16 = jnp.bfloat16

HEAD_DIM = 64
A_HEADS = 4
A_QK_DIM = 32
B_HEADS = 6
B_DILATIONS = (16, 4, 1)
B_HALF = 64
C_HEADS = 6
C_KV_HEADS = 2
C_HALF_WINDOW = 128
N_ATTN_HEADS = 16
LN_EPS = 1e-5
NEG = -1e30
LOG2E = math.log2(math.e)

LANES = 128
A_W = A_HEADS * HEAD_DIM
B_W = B_HEADS * HEAD_DIM
C_QW = C_HEADS * HEAD_DIM
C_KW = C_KV_HEADS * HEAD_DIM
IN_WIDTH = 3 * A_W + 3 * B_W + C_QW + 2 * C_KW
VMEM_LIMIT = 56 * 1024 * 1024

_S = 2.0 ** (-8.0 * np.arange(1, N_ATTN_HEADS + 1) / N_ATTN_HEADS)
SLOPES_C = [float(v) for v in _S[:C_HEADS]]
SLOPES_A = [float(v) for v in _S[C_HEADS:C_HEADS + A_HEADS]]
SLOPES_B = [float(v) for v in _S[C_HEADS + A_HEADS:]]


def _nt_dot(a, b):
    return lax.dot_general(a, b, (((1,), (1,)), ((), ())), preferred_element_type=F32)


def _lane_iota(shape):
    return lax.broadcasted_iota(jnp.int32, shape, len(shape) - 1)


def _mod_kernel(c_ref, w_ref, b_ref, o_ref):
    c = c_ref[...]
    a = (c * jax.nn.sigmoid(c)).astype(BF16)
    o_ref[0] = jnp.dot(a, w_ref[0].astype(BF16), preferred_element_type=F32) + b_ref[0]


def _modulation(c, w_ada, b_ada):
    depth, d, n = w_ada.shape
    bsz = c.shape[0]
    tn = 1536
    return pl.pallas_call(
        _mod_kernel,
        grid=(depth, n // tn),
        in_specs=[pl.BlockSpec((bsz, d), lambda l, j: (0, 0)),
                  pl.BlockSpec((1, d, tn), lambda l, j: (l, 0, j)),
                  pl.BlockSpec((1, 1, tn), lambda l, j: (l, 0, j))],
        out_specs=pl.BlockSpec((1, bsz, tn), lambda l, j: (l, 0, j)),
        out_shape=jax.ShapeDtypeStruct((depth, bsz, n), F32),
        compiler_params=pltpu.CompilerParams(dimension_semantics=("arbitrary", "arbitrary"),
                                             vmem_limit_bytes=VMEM_LIMIT),
        name="adaln_mod",
    )(c, w_ada, b_ada.reshape(depth, 1, n))


def _proj_kernel(x_ref, mod_ref, w_ref, cs_ref, o_ref, *, tn):
    sh = mod_ref[0, 0:1, :]
    sc = mod_ref[0, 1:2, :]
    h = (x_ref[0] * (1.0 + sc) + sh).astype(BF16)
    for n0 in range(0, o_ref.shape[2], tn):
        acc = jnp.dot(h, w_ref[:, n0:n0 + tn], preferred_element_type=F32)
        o_ref[0, :, n0:n0 + tn] = (acc * cs_ref[:, n0:n0 + tn]).astype(BF16)


def _in_projection(x, mod6, w_in_bf, layer, col_scale, tm=512):
    bsz, t, d = x.shape
    n = w_in_bf.shape[2]
    return pl.pallas_call(
        functools.partial(_proj_kernel, tn=512),
        grid=(bsz, t // tm),
        in_specs=[pl.BlockSpec((1, tm, d), lambda b, i: (b, i, 0)),
                  pl.BlockSpec((1, 6, d), lambda b, i: (b, 0, 0)),
                  pl.BlockSpec((None, d, n), lambda b, i: (layer, 0, 0), pipeline_mode=pl.Buffered(1)),
                  pl.BlockSpec((1, n), lambda b, i: (0, 0))],
        out_specs=pl.BlockSpec((1, tm, n), lambda b, i: (b, i, 0)),
        out_shape=jax.ShapeDtypeStruct((bsz, t, n), BF16),
        compiler_params=pltpu.CompilerParams(dimension_semantics=("parallel", "arbitrary"),
                                             vmem_limit_bytes=VMEM_LIMIT),
        name="in_proj",
    )(x, mod6, w_in_bf, col_scale)


def _attn_a_tables(tq, tk):
    rel = (np.arange(tq)[:, None] - np.arange(tk)[None, :]).astype(np.float64)
    tabs = [np.stack([-s * LOG2E * np.abs(rel + c * tq) for c in range(tk // tq)]) for s in SLOPES_A]
    return jnp.asarray(np.stack(tabs).astype(np.float32))


def _alibi_lane(head):
    return 3 * A_QK_DIM if head % 2 == 0 else 0


def _attn_a_alibi_lanes(seq):
    pos = np.arange(seq, dtype=np.float64)
    bf = jnp.bfloat16
    kside = np.zeros((A_HEADS, seq, LANES), np.float32)
    qside = np.zeros((A_HEADS, seq, LANES), np.float32)
    for h, s in enumerate(SLOPES_A):
        a0 = _alibi_lane(h)
        x = (pos * (s * LOG2E)).astype(np.float32)
        parts, rest = [], x
        for _ in range(3):
            part = np.asarray(rest.astype(bf), np.float32)
            parts.append(part)
            rest = (rest - part).astype(np.float32)
        for i, part in enumerate(parts):
            kside[h, :, a0 + i] = part
            qside[h, :, a0 + 3 + i] = -part
        kside[h, :, a0 + 3:a0 + 6] = 1.0
        qside[h, :, a0:a0 + 3] = 1.0
    return jnp.asarray(kside, BF16), jnp.asarray(qside, BF16)


def _attn_a_kernel(lam_ref, g_ref, tab_ref, kside_ref, qside_ref, q_ref, k_ref, v_ref, o_ref,
                   kaug, vaug, qv_sc, t_even, t_odd, acc_sc, *, tq, tk, seq, lambda_init):
    qi = pl.program_id(1)
    r = tk // tq
    nkv = seq // tk
    kd = qi // r
    var_d = qi % r
    i0 = pl.multiple_of(qi * tq, tq)
    lane = _lane_iota((1, LANES))
    first = lane < HEAD_DIM
    nmap = 2 * A_HEADS
    nlt = tk // LANES
    rows_per_build = 512

    def group_lanes(gm):
        lo = A_QK_DIM * (gm % 4)
        return (lane >= lo) & (lane < lo + A_QK_DIM)

    @pl.when(qi == 0)
    def _build_kv_copies():
        for rc in range(seq // rows_per_build):
            rows = slice(rc * rows_per_build, (rc + 1) * rows_per_build)
            for p in range(A_HEADS // 2):
                kp = k_ref[0, rows, p * LANES:(p + 1) * LANES]
                vp = v_ref[0, rows, p * LANES:(p + 1) * LANES]
                for a in range(2):
                    vaug[2 * p + a, rows, :] = jnp.where(first if a == 0 else jnp.logical_not(first), vp,
                                                         jnp.ones_like(vp))
                for a in range(2):
                    own = (lane >= 2 * a * A_QK_DIM) & (lane < (2 * a + 2) * A_QK_DIM)
                    kaug[2 * p + a, rows, :] = jnp.where(own, kp, kside_ref[2 * p + a, rows, :])

    for p in range(A_HEADS // 2):
        qp = q_ref[0, :, p * LANES:(p + 1) * LANES]
        for g in range(4):
            gm = 4 * p + g
            side = qside_ref[gm // 2, pl.ds(i0, tq), :]
            qv_sc[gm, 0] = jnp.where(group_lanes(gm), qp, jnp.zeros_like(qp))
            qv_sc[gm, 1] = jnp.where(group_lanes(gm), qp, side)
            qv_sc[gm, 2] = jnp.where(group_lanes(gm), qp, -side)

    chunk_ids = [kd] + [(kd + cc) & (nkv - 1) for cc in range(1, nkv)]
    starts = [pl.multiple_of(cid * tk, tk) for cid in chunk_ids]
    q_variant = [0] + [jnp.where(cid < kd, 1, 2) for cid in chunk_ids[1:]]

    def logits_chunk(gm, cc, mx):
        t_sc = t_odd if gm % 2 else t_even
        s = _nt_dot(qv_sc[gm, q_variant[cc]], kaug[gm // 2, pl.ds(starts[cc], tk), :])
        if cc == 0:
            s = s + tab_ref[gm // 2, var_d]
        t_sc[:, cc * tk:(cc + 1) * tk] = s
        cm = s[:, 0:LANES]
        for lt in range(1, nlt):
            cm = jnp.maximum(cm, s[:, lt * LANES:(lt + 1) * LANES])
        return cm if mx is None else jnp.maximum(mx, cm)

    def row_max(mx):
        return jnp.broadcast_to(jnp.max(mx, axis=1, keepdims=True), (tq, LANES))

    def pv_chunk(gm, cc, mb, acc):
        t_sc = t_odd if gm % 2 else t_even
        blocks = []
        for rb in range(0, tq, 64):
            blocks.append(jnp.concatenate(
                [jnp.exp2(t_sc[rb:rb + 64, cc * tk + lt * LANES:cc * tk + (lt + 1) * LANES]
                          - mb[rb:rb + 64, :]).astype(BF16) for lt in range(nlt)], axis=1))
        pv = jnp.dot(jnp.concatenate(blocks, axis=0), vaug[gm // 2, pl.ds(starts[cc], tk), :],
                     preferred_element_type=F32)
        return pv if acc is None else acc + pv

    mx = None
    for cc in range(nkv):
        mx = logits_chunk(0, cc, mx)
    mb = row_max(mx)
    for gm in range(nmap):
        mx, acc = None, None
        for cc in range(nkv):
            if gm + 1 < nmap:
                mx = logits_chunk(gm + 1, cc, mx)
            acc = pv_chunk(gm, cc, mb, acc)
        acc_sc[gm] = acc
        if gm + 1 < nmap:
            mb = row_max(mx)

    lf = lam_ref[...]
    lam_full = (jnp.exp(jnp.sum(lf[0:1] * lf[1:2], axis=(0, 1), keepdims=True))
                - jnp.exp(jnp.sum(lf[2:3] * lf[3:4], axis=(0, 1), keepdims=True)) + lambda_init)
    for p in range(A_HEADS // 2):
        d_heads = []
        for a in range(2):
            o_m = []
            for m2 in range(2):
                acc = acc_sc[4 * p + 2 * a + m2]
                o_m.append(acc / pltpu.roll(acc, HEAD_DIM, 1))
            d_heads.append(o_m[0] - lam_full * o_m[1])
        dp = jnp.where(first, d_heads[0], d_heads[1])
        sq = dp * dp
        ms0 = jnp.sum(jnp.where(first, sq, 0.0), axis=1, keepdims=True) * (1.0 / HEAD_DIM)
        ms1 = jnp.sum(jnp.where(first, 0.0, sq), axis=1, keepdims=True) * (1.0 / HEAD_DIM)
        ms = jnp.where(first, ms0, ms1)
        out = dp * lax.rsqrt(ms + LN_EPS) * g_ref[...] * (1.0 - lambda_init)
        o_ref[0, :, p * LANES:(p + 1) * LANES] = out.astype(o_ref.dtype)


def _attention_a(proj, lam_l, g2, lambda_init, tq=512, tk=512):
    bsz, t, _ = proj.shape
    tab = _attn_a_tables(tq, tk)
    kside, qside = _attn_a_alibi_lanes(t)
    nmap = 2 * A_HEADS
    kern = functools.partial(_attn_a_kernel, tq=tq, tk=tk, seq=t, lambda_init=lambda_init)
    return pl.pallas_call(
        kern,
        grid=(bsz, t // tq),
        in_specs=[pl.BlockSpec((4, A_QK_DIM), lambda b, i: (0, 0)),
                  pl.BlockSpec((1, LANES), lambda b, i: (0, 0)),
                  pl.BlockSpec(tab.shape, lambda b, i: (0, 0, 0, 0), pipeline_mode=pl.Buffered(1)),
                  pl.BlockSpec(kside.shape, lambda b, i: (0, 0, 0), pipeline_mode=pl.Buffered(1)),
                  pl.BlockSpec(qside.shape, lambda b, i: (0, 0, 0), pipeline_mode=pl.Buffered(1)),
                  pl.BlockSpec((1, tq, A_W), lambda b, i: (b, i, 0)),
                  pl.BlockSpec((1, t, A_W), lambda b, i: (b, 0, 1)),
                  pl.BlockSpec((1, t, A_W), lambda b, i: (b, 0, 2))],
        out_specs=pl.BlockSpec((1, tq, A_W), lambda b, i: (b, i, 0)),
        out_shape=jax.ShapeDtypeStruct((bsz, t, A_W), BF16),
        scratch_shapes=[pltpu.VMEM((A_HEADS, t, LANES), BF16),
                        pltpu.VMEM((A_HEADS, t, LANES), BF16),
                        pltpu.VMEM((nmap, 3, tq, LANES), BF16),
                        pltpu.VMEM((tq, t), F32),
                        pltpu.VMEM((tq, t), F32),
                        pltpu.VMEM((nmap, tq, LANES), F32)],
        compiler_params=pltpu.CompilerParams(dimension_semantics=("parallel", "arbitrary"),
                                             vmem_limit_bytes=VMEM_LIMIT),
        name="attn_diff",
    )(lam_l, g2, tab, kside, qside, proj, proj, proj)


def _attn_b_kernel(sl_ref, q_ref, k_ref, v_ref, o_ref, qf, kf, vf, q4, k4, v4, m_st, l_st, o_st, bias_sc,
                   *, seq, tq, win, unroll):
    lane = _lane_iota((1, LANES))
    first = lane < HEAD_DIM
    halves = (first, jnp.logical_not(first))
    chunk = 512

    def prep(c, carry):
        r0 = pl.multiple_of(c * chunk, chunk)
        rows = pl.ds(r0, chunk)
        qf[rows, :] = q_ref[0, rows, :].astype(F32)
        kf[rows, :] = k_ref[0, rows, :].astype(F32)
        vf[rows, :] = v_ref[0, rows, :].astype(F32)
        return carry

    lax.fori_loop(0, seq // chunk, prep, 0)

    len4 = seq // 4
    for src_ref, dst_ref in ((qf, q4), (kf, k4), (vf, v4)):
        for r4 in range(4):
            for c0 in range(0, len4, chunk):
                dst_ref[r4 * len4 + c0:r4 * len4 + c0 + chunk, :] = src_ref[pl.ds(r4 + 4 * c0, chunk, stride=4), :]

    relb = (lax.broadcasted_iota(jnp.int32, (tq, win), 0) - lax.broadcasted_iota(jnp.int32, (tq, win), 1))

    for d in B_DILATIONS:
        length = seq // d
        nblk = length // tq

        for case in range(3):
            adist = jnp.abs(relb + case * B_HALF)
            adf = adist.astype(F32)
            for a in range(2):
                sl = sl_ref[0, a:a + 1, 0:1] * float(d)
                bias_sc[case, a] = jnp.where(adist <= B_HALF, -sl * adf, NEG)

        def state_rows(res, j, size, d=d):
            return pl.ds(j, size) if d == 1 else pl.ds(res + j * d, size, stride=d)

        def data_rows(res, j, size, d=d):
            if d == 1:
                return pl.ds(j, size)
            if d == 4:
                return pl.ds(res * len4 + j, size)
            return pl.ds((res % 4) * len4 + res // 4 + j * (d // 4), size, stride=d // 4)

        q_src, k_src, v_src = (qf, kf, vf) if d == 1 else (q4, k4, v4)

        merge = d != B_DILATIONS[0]

        def body(it, carry, length=length, nblk=nblk, state_rows=state_rows, data_rows=data_rows,
                 q_src=q_src, k_src=k_src, v_src=v_src, merge=merge):
            rqs, cases, qs, ks, vs = [], [], [], [], []
            for u in range(unroll):
                idx = it * unroll + u
                res = idx // nblk
                j0 = (idx % nblk) * tq
                start = jnp.clip(j0 - B_HALF, 0, length - win)
                rqs.append(state_rows(res, j0, tq))
                cases.append((j0 - start) // B_HALF)
                qs.append(q_src[data_rows(res, j0, tq), :])
                ks.append(k_src[data_rows(res, start, win), :].astype(BF16))
                vs.append(v_src[data_rows(res, start, win), :])
            items = [(u, a) for u in range(unroll) for a in range(2)]
            ss = [_nt_dot(jnp.where(halves[a], qs[u], 0.0).astype(BF16), ks[u]) for u, a in items]
            ts = [ss[i] + bias_sc[cases[u], a] for i, (u, a) in enumerate(items)]
            mbs = [jnp.max(t, axis=1, keepdims=True) for t in ts]
            prs = [jnp.exp2(t - mb).astype(BF16) for t, mb in zip(ts, mbs)]
            pvs = [jnp.dot(prs[i], jnp.where(halves[a], vs[u], 1.0).astype(BF16), preferred_element_type=F32)
                   for i, (u, a) in enumerate(items)]
            if merge:
                m_prev = [m_st[rq, :] for rq in rqs]
                l_prev = [l_st[rq, :] for rq in rqs]
                o_prev = [o_st[rq, :] for rq in rqs]
            for u in range(unroll):
                pv0, pv1 = pvs[2 * u], pvs[2 * u + 1]
                o_blk = jnp.where(first, pv0, pv1)
                l_blk = pltpu.roll(jnp.where(first, pv1, pv0), HEAD_DIM, 1)
                m_blk = jnp.where(first, mbs[2 * u], mbs[2 * u + 1])
                if not merge:
                    l_st[rqs[u], :] = l_blk
                    o_st[rqs[u], :] = o_blk
                    m_st[rqs[u], :] = jnp.broadcast_to(m_blk, (tq, LANES))
                    continue
                m_new = jnp.maximum(m_prev[u], m_blk)
                a_prev = jnp.exp2(m_prev[u] - m_new)
                a_blk = jnp.exp2(m_blk - m_new)
                l_st[rqs[u], :] = a_prev * l_prev[u] + a_blk * l_blk
                o_st[rqs[u], :] = a_prev * o_prev[u] + a_blk * o_blk
                m_st[rqs[u], :] = m_new
            return carry

        lax.fori_loop(0, seq // tq // unroll, body, 0)

    def fin(c, carry):
        r0 = pl.multiple_of(c * chunk, chunk)
        rows = pl.ds(r0, chunk)
        o_ref[0, rows, :] = (o_st[rows, :] / l_st[rows, :]).astype(o_ref.dtype)
        return carry

    lax.fori_loop(0, seq // chunk, fin, 0)


def _attention_b(proj, tq=128, win=256, unroll=32):
    bsz, t, _ = proj.shape
    npair = B_HEADS // 2
    sl = np.zeros((npair, 8, LANES), np.float32)
    for h, s in enumerate(SLOPES_B):
        sl[h // 2, h % 2, :] = s * LOG2E
    qb0 = 3 * A_W // LANES
    kern = functools.partial(_attn_b_kernel, seq=t, tq=tq, win=win, unroll=unroll)
    seq_spec = lambda off: pl.BlockSpec((1, t, LANES), lambda b, p, off=off: (b, 0, off + p))
    return pl.pallas_call(
        kern,
        grid=(bsz, npair),
        in_specs=[pl.BlockSpec((1, 8, LANES), lambda b, p: (p, 0, 0)),
                  seq_spec(qb0), seq_spec(qb0 + B_W // LANES), seq_spec(qb0 + 2 * B_W // LANES)],
        out_specs=pl.BlockSpec((1, t, LANES), lambda b, p: (b, 0, p)),
        out_shape=jax.ShapeDtypeStruct((bsz, t, B_W), BF16),
        scratch_shapes=[pltpu.VMEM((t, LANES), F32) for _ in range(9)] + [pltpu.VMEM((3, 2, tq, win), F32)],
        compiler_params=pltpu.CompilerParams(dimension_semantics=("parallel", "arbitrary"),
                                             vmem_limit_bytes=VMEM_LIMIT),
        name="attn_dilated",
    )(jnp.asarray(sl), proj, proj, proj)


def _attn_c_tables(tq, win):
    rel = np.arange(tq)[:, None] - np.arange(win)[None, :]
    tabs = []
    for case in range(3):
        dist = np.abs(rel + case * C_HALF_WINDOW).astype(np.float64)
        tabs.append(np.stack([np.where(dist <= C_HALF_WINDOW, -s * LOG2E * dist, NEG) for s in SLOPES_C]))
    return jnp.asarray(np.stack(tabs).astype(np.float32))


def _attn_c_kernel(sink_ref, tab_ref, q_ref, k_ref, v_ref, o_ref, *, seq, tq, win, nsub):
    qi = pl.program_id(1)
    lane = _lane_iota((1, LANES))
    first = lane < HEAD_DIM
    halves = (first, jnp.logical_not(first))
    rep = C_HEADS // C_KV_HEADS
    kv_of = [h // rep for h in range(C_HEADS)]
    items = [(sb, h) for sb in range(nsub) for h in range(C_HEADS)]
    cases, k_same, k_swap, v_aug = [], [], [], []
    for sb in range(nsub):
        i0 = (qi * nsub + sb) * tq
        start = pl.multiple_of(jnp.clip(i0 - C_HALF_WINDOW, 0, seq - win), C_HALF_WINDOW)
        cases.append((i0 - start) // C_HALF_WINDOW)
        k2 = k_ref[0, pl.ds(start, win), :].astype(F32)
        v2 = v_ref[0, pl.ds(start, win), :]
        k_same.append(k2.astype(BF16))
        k_swap.append(pltpu.roll(k2, HEAD_DIM, 1).astype(BF16))
        v_aug.append([jnp.where(halves[g], v2, jnp.ones_like(v2)) for g in range(C_KV_HEADS)])

    def q_masked(sb, h):
        qp = q_ref[0, sb * tq:(sb + 1) * tq, (h // 2) * LANES:(h // 2 + 1) * LANES]
        return jnp.where(halves[h % 2], qp, jnp.zeros_like(qp))

    ss = [_nt_dot(q_masked(sb, h), k_same[sb] if h % 2 == kv_of[h] else k_swap[sb]) for sb, h in items]
    ts = [ss[i] + tab_ref[cases[sb], h] for i, (sb, h) in enumerate(items)]
    ms = [jnp.max(t, axis=1, keepdims=True) for t in ts]
    prs = [jnp.exp2(t - m).astype(BF16) for t, m in zip(ts, ms)]
    pvs = [jnp.dot(prs[i], v_aug[sb][kv_of[h]], preferred_element_type=F32) for i, (sb, h) in enumerate(items)]
    swp = [pltpu.roll(pv, HEAD_DIM, 1) for pv in pvs]
    outs = [pvs[i] if h % 2 == kv_of[h] else swp[i] for i, (sb, h) in enumerate(items)]
    dens = [swp[i] if h % 2 == kv_of[h] else pvs[i] for i, (sb, h) in enumerate(items)]
    for sb in range(nsub):
        for p in range(C_HEADS // 2):
            i0, i1 = sb * C_HEADS + 2 * p, sb * C_HEADS + 2 * p + 1
            m_pair = jnp.where(first, ms[i0], ms[i1])
            l_pair = jnp.where(first, dens[i0], dens[i1])
            o_pair = jnp.where(first, outs[i0], outs[i1])
            sk = sink_ref[:, p * LANES:(p + 1) * LANES] * LOG2E
            mx = jnp.maximum(m_pair, sk)
            a_ = jnp.exp2(m_pair - mx)
            out = o_pair * a_ / (l_pair * a_ + jnp.exp2(sk - mx))
            o_ref[0, sb * tq:(sb + 1) * tq, p * LANES:(p + 1) * LANES] = out.astype(o_ref.dtype)


def _attention_c(proj, sink_lanes, tq=128, win=384, nsub=8):
    bsz, t, _ = proj.shape
    tab = _attn_c_tables(tq, win)
    kern = functools.partial(_attn_c_kernel, seq=t, tq=tq, win=win, nsub=nsub)
    qc0 = (3 * A_W + 3 * B_W) // C_QW
    kc0 = (3 * A_W + 3 * B_W + C_QW) // C_KW
    return pl.pallas_call(
        kern,
        grid=(bsz, t // (tq * nsub)),
        in_specs=[pl.BlockSpec((1, C_QW), lambda b, i: (0, 0)),
                  pl.BlockSpec(tab.shape, lambda b, i: (0, 0, 0, 0), pipeline_mode=pl.Buffered(1)),
                  pl.BlockSpec((1, tq * nsub, C_QW), lambda b, i: (b, i, qc0)),
                  pl.BlockSpec((1, t, C_KW), lambda b, i: (b, 0, kc0)),
                  pl.BlockSpec((1, t, C_KW), lambda b, i: (b, 0, kc0 + 1))],
        out_specs=pl.BlockSpec((1, tq * nsub, C_QW), lambda b, i: (b, i, 0)),
        out_shape=jax.ShapeDtypeStruct((bsz, t, C_QW), BF16),
        compiler_params=pltpu.CompilerParams(dimension_semantics=("parallel", "arbitrary"),
                                             vmem_limit_bytes=VMEM_LIMIT),
        name="attn_window",
    )(sink_lanes, tab, proj, proj, proj)


def _layer_norm(y, g, b):
    mu = jnp.mean(y, axis=-1, keepdims=True)
    yc = y - mu
    var = jnp.mean(yc * yc, axis=-1, keepdims=True)
    return yc * lax.rsqrt(var + LN_EPS) * g + b


def _tail_kernel(oa_ref, ob_ref, oc_ref, x_ref, mod_ref, wo_ref, wgu_ref, wd_ref, lng_ref, lnb_ref, o_ref,
                 *, alpha, hidden, ck):
    mix = (jnp.dot(oa_ref[0], wo_ref[0:A_W, :], preferred_element_type=F32)
           + jnp.dot(ob_ref[0], wo_ref[A_W:A_W + B_W, :], preferred_element_type=F32)
           + jnp.dot(oc_ref[0], wo_ref[A_W + B_W:, :], preferred_element_type=F32))
    g1 = mod_ref[0, 2:3, :]
    x1 = _layer_norm(alpha * x_ref[0] + (1.0 + g1) * mix, lng_ref[0:1, :], lnb_ref[0:1, :])
    sh2 = mod_ref[0, 3:4, :]
    sc2 = mod_ref[0, 4:5, :]
    g2 = mod_ref[0, 5:6, :]
    h2 = (x1 * (1.0 + sc2) + sh2).astype(BF16)
    acc = jnp.zeros(x1.shape, F32)
    for c0 in range(0, hidden, ck):
        gate = jnp.dot(h2, wgu_ref[:, c0:c0 + ck], preferred_element_type=F32)
        up = jnp.dot(h2, wgu_ref[:, hidden + c0:hidden + c0 + ck], preferred_element_type=F32)
        act = (gate * jax.nn.sigmoid(gate) * up).astype(BF16)
        acc = acc + jnp.dot(act, wd_ref[c0:c0 + ck, :], preferred_element_type=F32)
    o_ref[0] = _layer_norm(alpha * x1 + (1.0 + g2) * acc, lng_ref[1:2, :], lnb_ref[1:2, :])


def _tail(oa, ob, oc, x, mod6, wo_bf, wgu_bf, wd_bf, layer, ln_g, ln_b, alpha, tm=512):
    bsz, t, d = x.shape
    hidden = wd_bf.shape[1]
    kern = functools.partial(_tail_kernel, alpha=alpha, hidden=hidden, ck=256)
    row = lambda w: pl.BlockSpec((1, tm, w), lambda b, i: (b, i, 0))
    whole = lambda shp: pl.BlockSpec((None,) + tuple(shp[1:]), lambda b, i: (layer, 0, 0),
                                     pipeline_mode=pl.Buffered(1))
    return pl.pallas_call(
        kern,
        grid=(bsz, t // tm),
        in_specs=[row(A_W), row(B_W), row(C_QW), row(d),
                  pl.BlockSpec((1, 6, d), lambda b, i: (b, 0, 0)),
                  whole(wo_bf.shape), whole(wgu_bf.shape), whole(wd_bf.shape),
                  pl.BlockSpec((2, d), lambda b, i: (0, 0)),
                  pl.BlockSpec((2, d), lambda b, i: (0, 0))],
        out_specs=row(d),
        out_shape=jax.ShapeDtypeStruct((bsz, t, d), F32),
        compiler_params=pltpu.CompilerParams(dimension_semantics=("parallel", "arbitrary"),
                                             vmem_limit_bytes=VMEM_LIMIT),
        name="outproj_ffn",
    )(oa, ob, oc, x, mod6, wo_bf, wgu_bf, wd_bf, ln_g, ln_b)


def _q_col_scale():
    cs = np.ones((1, IN_WIDTH), np.float32)
    cs[0, 0:A_W] = A_QK_DIM ** -0.5 * LOG2E
    cs[0, 3 * A_W:3 * A_W + B_W] = HEAD_DIM ** -0.5 * LOG2E
    cs[0, 3 * A_W + 3 * B_W:3 * A_W + 3 * B_W + C_QW] = HEAD_DIM ** -0.5 * LOG2E
    return jnp.asarray(cs)


def kernel(x, c, w_ada, b_ada, w_in, lam, subln_g, sink, w_out, ln_g, ln_b, w_gu, w_down):
    depth = w_ada.shape[0]
    bsz, _, d = x.shape
    alpha = (2 * depth) ** 0.25
    mod = _modulation(c, w_ada, b_ada).reshape(depth, bsz, 6, d)
    col_scale = _q_col_scale()
    w_in_bf, w_out_bf, w_gu_bf, w_down_bf = (w.astype(BF16) for w in (w_in, w_out, w_gu, w_down))
    for layer in range(depth):
        lambda_init = 0.8 - 0.6 * math.exp(-0.3 * layer)
        proj = _in_projection(x, mod[layer], w_in_bf, layer, col_scale)
        g2 = jnp.tile(subln_g[layer], 2).reshape(1, LANES)
        oa = _attention_a(proj, lam[layer], g2, lambda_init)
        ob = _attention_b(proj)
        oc = _attention_c(proj, jnp.repeat(sink[layer], HEAD_DIM).reshape(1, C_QW))
        x = _tail(oa, ob, oc, x, mod[layer], w_out_bf, w_gu_bf, w_down_bf, layer, ln_g[layer], ln_b[layer], alpha)
    return x
```
